```python
import jax, jax.numpy as jnp
from jax import lax
import numpy as np

D_MODEL = 2048
BATCH = 16
SEQ = 2048
DEPTH = 1
DEC_BATCH = 32
DEC_SEQ = 8
PAST_LEN = 16384
PAGE_SIZE = 128

HEAD_DIM = 64
N_HEADS_ATT = 12
N_HEADS_RWKV = 20
D_ATT = N_HEADS_ATT * HEAD_DIM
D_RWKV = N_HEADS_RWKV * HEAD_DIM
DIL_BRANCHES = ((128, 1), (512, 4), (2048, 16))
MAX_WINDOW = 2048
BAND_BLOCK = 128
ROT_DIM = HEAD_DIM // 4
ROPE_THETA = 500000.0
LORA_DECAY = 96
LORA_ICLR = 96
LORA_GATE = 256
GN_EPS = 64e-5
RMS_EPS = 1e-6
N_GROUPS = 8
EXPERTS_PER_GROUP = 8
N_EXPERTS = N_GROUPS * EXPERTS_PER_GROUP
TOP_K_FINE = 2
D_EXPERT = 1024
MOE_BLOCK = 128
D_RWKV_IN = 3 * D_RWKV + LORA_DECAY + LORA_ICLR + LORA_GATE
D_IN_PROJ = 3 * D_ATT + D_RWKV_IN

kernel_name = 'dilated_rwkv7_hier_moe_step'


def rms_norm(x, g):
    x32 = x.astype(jnp.float32)
    y = x32 * lax.rsqrt(jnp.mean(x32 * x32, axis=-1, keepdims=True) + RMS_EPS)
    return (y * g.astype(jnp.float32)).astype(x.dtype)


def partial_rope(x, pos):
    half = ROT_DIM // 2
    inv_freq = ROPE_THETA ** (-2.0 * jnp.arange(half, dtype=jnp.float32) / ROT_DIM)
    ang = pos[:, None] * inv_freq[None, :]
    cos = jnp.cos(ang)[None, :, None, :]
    sin = jnp.sin(ang)[None, :, None, :]
    x32 = x.astype(jnp.float32)
    x1 = x32[..., :half]
    x2 = x32[..., half:ROT_DIM]
    out = jnp.concatenate([x1 * cos - x2 * sin, x2 * cos + x1 * sin, x32[..., ROT_DIM:]], axis=-1)
    return out.astype(x.dtype)


def _band_mask(n_blocks, n_back):
    qi = jnp.arange(BAND_BLOCK)[:, None]
    kj = jnp.arange(2 * BAND_BLOCK)[None, :]
    dist = qi + BAND_BLOCK - kj
    in_band = (dist >= 0) & (dist <= n_back)
    blk = jnp.arange(n_blocks)[:, None, None]
    exists = blk * BAND_BLOCK + kj[None] - BAND_BLOCK >= 0
    return in_band[None] & exists


def dilated_branch_prompt(q, k, v, window, dil):
    b, s, h, dh = q.shape
    n_back = window // dil
    sub = s // dil
    n_blk = -(-sub // BAND_BLOCK)
    pad = n_blk * BAND_BLOCK - sub

    def to_blocks(z):
        z = z.reshape(b, sub, dil, h, dh).transpose(0, 2, 1, 3, 4).reshape(b * dil, sub, h, dh)
        z = jnp.pad(z, ((0, 0), (0, pad), (0, 0), (0, 0)))
        return z.reshape(b * dil, n_blk, BAND_BLOCK, h, dh)

    def with_prev(z):
        prev = jnp.pad(z[:, :-1], ((0, 0), (1, 0), (0, 0), (0, 0), (0, 0)))
        return jnp.concatenate([prev, z], axis=2)

    qb, kb, vb = to_blocks(q), to_blocks(k), to_blocks(v)
    kw, vw = with_prev(kb), with_prev(vb)
    scores = jnp.einsum('znqhd,znkhd->znhqk', qb, kw,
                        preferred_element_type=jnp.float32) * (dh ** -0.5)
    mask = _band_mask(n_blk, n_back)[None, :, None]
    scores = jnp.where(mask, scores, -jnp.inf)
    m = jnp.max(scores, axis=-1, keepdims=True)
    p = jnp.exp(scores - m)
    den = jnp.sum(p, axis=-1, keepdims=True)
    o = jnp.einsum('znhqk,znkhd->znqhd', (p / den).astype(v.dtype), vw)
    lse = (m + jnp.log(den))[..., 0].transpose(0, 1, 3, 2)

    def from_blocks(z):
        tail = z.shape[3:]
        z = z.reshape((b * dil, n_blk * BAND_BLOCK) + tail)[:, :sub]
        z = jnp.moveaxis(z.reshape((b, dil, sub) + tail), 1, 2)
        return z.reshape((b, s) + tail)

    return from_blocks(o), from_blocks(lse)


def dilated_branch_sample(q, k_all, v_all, window, dil):
    b, t, h, dh = q.shape
    n_ctx = k_all.shape[1] - t
    steps = jnp.arange(window // dil + 1)
    rows = n_ctx + jnp.arange(t)[:, None] - dil * steps[None, :]
    valid = rows >= 0
    rows = jnp.maximum(rows, 0)
    kg = k_all[:, rows]
    vg = v_all[:, rows]
    scores = jnp.einsum('bthd,btjhd->bthj', q, kg,
                        preferred_element_type=jnp.float32) * (dh ** -0.5)
    scores = jnp.where(valid[None, :, None, :], scores, -jnp.inf)
    m = jnp.max(scores, axis=-1, keepdims=True)
    p = jnp.exp(scores - m)
    den = jnp.sum(p, axis=-1, keepdims=True)
    o = jnp.einsum('bthj,btjhd->bthd', (p / den).astype(v_all.dtype), vg)
    return o, (m + jnp.log(den))[..., 0]


def merge_branches(branches):
    o = jnp.stack([bo for bo, _ in branches])
    lse = jnp.stack([bl for _, bl in branches])
    wts = jax.nn.softmax(lse, axis=0)
    return jnp.einsum('gbth,gbthd->bthd', wts.astype(o.dtype), o)


def rwkv7_time_mix(p_cur, p_prev, wkv0, tmix_mu, decay_w0, decay_up, iclr_a0, iclr_up, gate_up,
                   k_k, k_a, r_k, gn_w, gn_b):
    b, t, _ = p_cur.shape
    f32 = jnp.float32
    cur = p_cur.astype(f32)
    prev = p_prev.astype(f32)
    xm = cur + (prev - cur) * tmix_mu
    c1, c2, c3 = D_RWKV, 2 * D_RWKV, 3 * D_RWKV
    c4 = c3 + LORA_DECAY
    c5 = c4 + LORA_ICLR
    r, k, v, wd, ad, gd = jnp.split(xm, [c1, c2, c3, c4, c5], axis=-1)
    w = -jax.nn.softplus(-(decay_w0 + jnp.tanh(wd) @ decay_up)) - 0.5
    decay = jnp.exp(-jnp.exp(w))
    a = jax.nn.sigmoid(iclr_a0 + ad @ iclr_up)
    g = jax.nn.sigmoid(gd) @ gate_up
    kk = k * k_k
    k = k * (1.0 + (a - 1.0) * k_a)
    heads = lambda z: z.reshape(b, t, N_HEADS_RWKV, HEAD_DIM)
    r, k, v, a, decay, kk = heads(r), heads(k), heads(v), heads(a), heads(decay), heads(kk)
    kk = kk * lax.rsqrt(jnp.maximum(jnp.sum(kk * kk, axis=-1, keepdims=True), 1e-24))

    def step(S, inp):
        r_t, k_t, v_t, kk_t, a_t, d_t = inp
        sa = jnp.einsum('bhij,bhj->bhi', S, -kk_t)
        S = (S * d_t[:, :, None, :] + sa[..., None] * (kk_t * a_t)[:, :, None, :]
             + v_t[..., None] * k_t[:, :, None, :])
        return S, jnp.einsum('bhij,bhj->bhi', S, r_t)

    xs = tuple(jnp.swapaxes(z, 0, 1) for z in (r, k, v, kk, a, decay))
    wkv, o = lax.scan(step, wkv0.astype(f32), xs)
    o = jnp.swapaxes(o, 0, 1)
    mean = jnp.mean(o, axis=-1, keepdims=True)
    var = jnp.mean(jnp.square(o - mean), axis=-1, keepdims=True)
    o = ((o - mean) * lax.rsqrt(var + GN_EPS) * gn_w.reshape(N_HEADS_RWKV, HEAD_DIM)
         + gn_b.reshape(N_HEADS_RWKV, HEAD_DIM))
    o = o + jnp.sum(r * k * r_k, axis=-1, keepdims=True) * v
    o = o.reshape(b, t, D_RWKV) * g
    return o.astype(p_cur.dtype), wkv


def hier_moe(h, router_group, router_expert, moe_w_gate, moe_w_up, moe_w_down):
    n, d = h.shape
    h32 = h.astype(jnp.float32)
    lg = h32 @ router_group.astype(jnp.float32)
    pg = jax.nn.softmax(lg, axis=-1)
    grp = jnp.argmax(lg, axis=-1)
    p_grp = jnp.take_along_axis(pg, grp[:, None], axis=-1)
    le = (h32 @ router_expert.astype(jnp.float32)).reshape(n, N_GROUPS, EXPERTS_PER_GROUP)
    le = jnp.take_along_axis(le, grp[:, None, None], axis=1)[:, 0]
    top_p, top_i = lax.top_k(jax.nn.softmax(le, axis=-1), TOP_K_FINE)
    gate = p_grp * top_p / jnp.sum(top_p, axis=-1, keepdims=True)
    eid = (grp[:, None] * EXPERTS_PER_GROUP + top_i).reshape(-1)
    tok = jnp.repeat(jnp.arange(n, dtype=jnp.int32), TOP_K_FINE)
    wgt = gate.reshape(-1)
    m = n * TOP_K_FINE
    order = jnp.argsort(eid)
    e_s, tok_s, w_s = eid[order], tok[order], wgt[order]
    counts = jnp.zeros((N_EXPERTS,), jnp.int32).at[eid].add(1)
    padded = (counts + MOE_BLOCK - 1) // MOE_BLOCK * MOE_BLOCK
    pend = jnp.cumsum(padded)
    pstart = pend - padded
    start = jnp.cumsum(counts) - counts
    dest = pstart[e_s] + jnp.arange(m, dtype=jnp.int32) - start[e_s]
    n_blocks = -(-(m + N_EXPERTS * (MOE_BLOCK - 1)) // MOE_BLOCK)
    cap = n_blocks * MOE_BLOCK
    buf_tok = jnp.full((cap,), n, jnp.int32).at[dest].set(tok_s)
    h_pad = jnp.concatenate([h, jnp.zeros((1, d), h.dtype)], axis=0)
    xb = h_pad[buf_tok].reshape(n_blocks, MOE_BLOCK, d)
    blk_e = jnp.minimum(jnp.searchsorted(pend, jnp.arange(n_blocks) * MOE_BLOCK, side='right'),
                        N_EXPERTS - 1)

    def expert_block(args):
        xblk, e = args
        return (jax.nn.silu(xblk @ moe_w_gate[e]) * (xblk @ moe_w_up[e])) @ moe_w_down[e]

    yb = lax.map(expert_block, (xb, blk_e)).reshape(cap, d)
    out = jnp.zeros((n, d), jnp.float32).at[tok_s].add(yb[dest].astype(jnp.float32) * w_s[:, None])
    return out.astype(h.dtype)


def hybrid_layer(x, pos, shift_row, wkv0, k_ctx, v_ctx,
                 norm1_g, w_in, q_norm_g, k_norm_g, tmix_mu, decay_w0, decay_up, iclr_a0, iclr_up,
                 gate_up, k_k, k_a, r_k, gn_w, gn_b, w_out, norm2_g, router_group, router_expert,
                 moe_w_gate, moe_w_up, moe_w_down):
    b, t, d = x.shape
    h = rms_norm(x, norm1_g)
    h_ext = jnp.concatenate([shift_row[:, None, :].astype(h.dtype), h], axis=1)
    proj = h_ext @ w_in
    q, k, v = jnp.split(proj[:, 1:, :3 * D_ATT], 3, axis=-1)
    to_heads = lambda z: z.reshape(b, t, N_HEADS_ATT, HEAD_DIM)
    q = partial_rope(rms_norm(to_heads(q), q_norm_g), pos)
    k = partial_rope(rms_norm(to_heads(k), k_norm_g), pos)
    v = to_heads(v)
    if k_ctx is None:
        branches = [dilated_branch_prompt(q, k, v, wdw, dil) for wdw, dil in DIL_BRANCHES]
    else:
        k_all = jnp.concatenate([k_ctx.astype(k.dtype), k], axis=1)
        v_all = jnp.concatenate([v_ctx.astype(v.dtype), v], axis=1)
        branches = [dilated_branch_sample(q, k_all, v_all, wdw, dil) for wdw, dil in DIL_BRANCHES]
    o_att = merge_branches(branches).reshape(b, t, D_ATT)
    rw = proj[..., 3 * D_ATT:]
    o_rwkv, wkv = rwkv7_time_mix(rw[:, 1:], rw[:, :-1], wkv0, tmix_mu, decay_w0, decay_up, iclr_a0,
                                 iclr_up, gate_up, k_k, k_a, r_k, gn_w, gn_b)
    x = x + jnp.concatenate([o_att, o_rwkv], axis=-1) @ w_out
    h2 = rms_norm(x, norm2_g)
    x = x + hier_moe(h2.reshape(b * t, d), router_group, router_expert,
                     moe_w_gate, moe_w_up, moe_w_down).reshape(b, t, d)
    return x, k, v, wkv, h[:, -1]


def setup_inputs(seed: int = 0) -> dict:
    key = jax.random.key(seed)
    ks = iter(jax.random.split(key, 40))
    nrm = lambda shape, scale: scale * jax.random.normal(next(ks), shape, jnp.float32)
    L = DEPTH
    wbuf = min(MAX_WINDOW, PAST_LEN)
    return {
        'x_prompt': nrm((BATCH, SEQ, D_MODEL), 1.0),
        'x_sample': nrm((DEC_BATCH, DEC_SEQ, D_MODEL), 1.0),
        'cache_k_win': nrm((L, DEC_BATCH, wbuf, N_HEADS_ATT, HEAD_DIM), 1.0),
        'cache_v_win': nrm((L, DEC_BATCH, wbuf, N_HEADS_ATT, HEAD_DIM), 1.0),
        'state_wkv': nrm((L, DEC_BATCH, N_HEADS_RWKV, HEAD_DIM, HEAD_DIM), 0.3),
        'state_shift': nrm((L, DEC_BATCH, D_MODEL), 1.0),
        'norm1_g': 1.0 + nrm((L, D_MODEL), 0.02),
        'w_in': nrm((L, D_MODEL, D_IN_PROJ), D_MODEL ** -0.5),
        'q_norm_g': 1.0 + nrm((L, HEAD_DIM), 0.02),
        'k_norm_g': 1.0 + nrm((L, HEAD_DIM), 0.02),
        'tmix_mu': jax.random.uniform(next(ks), (L, D_RWKV_IN), jnp.float32),
        'decay_w0': jax.random.uniform(next(ks), (L, D_RWKV), jnp.float32, minval=-6.5, maxval=-1.5),
        'decay_up': nrm((L, LORA_DECAY, D_RWKV), 0.5 * LORA_DECAY ** -0.5),
        'iclr_a0': nrm((L, D_RWKV), 0.5),
        'iclr_up': nrm((L, LORA_ICLR, D_RWKV), LORA_ICLR ** -0.5),
        'gate_up': nrm((L, LORA_GATE, D_RWKV), LORA_GATE ** -0.5),
        'k_k': 0.85 + nrm((L, D_RWKV), 0.05),
        'k_a': 1.0 + nrm((L, D_RWKV), 0.05),
        'r_k': nrm((L, N_HEADS_RWKV, HEAD_DIM), 0.1),
        'gn_w': 1.0 + nrm((L, D_RWKV), 0.02),
        'gn_b': nrm((L, D_RWKV), 0.02),
        'w_out': nrm((L, D_MODEL, D_MODEL), D_MODEL ** -0.5),
        'norm2_g': 1.0 + nrm((L, D_MODEL), 0.02),
        'router_group': nrm((L, D_MODEL, N_GROUPS), D_MODEL ** -0.5),
        'router_expert': nrm((L, D_MODEL, N_EXPERTS), D_MODEL ** -0.5),
        'moe_w_gate': nrm((L, N_EXPERTS, D_MODEL, D_EXPERT), D_MODEL ** -0.5),
        'moe_w_up': nrm((L, N_EXPERTS, D_MODEL, D_EXPERT), D_MODEL ** -0.5),
        'moe_w_down': nrm((L, N_EXPERTS, D_EXPERT, D_MODEL), D_EXPERT ** -0.5),
    }


def reference(x_prompt, x_sample, cache_k_win, cache_v_win, state_wkv, state_shift,
              norm1_g, w_in, q_norm_g, k_norm_g, tmix_mu, decay_w0, decay_up, iclr_a0, iclr_up,
              gate_up, k_k, k_a, r_k, gn_w, gn_b, w_out, norm2_g, router_group, router_expert,
              moe_w_gate, moe_w_up, moe_w_down):
    b_p, s_p, d = x_prompt.shape
    t_s = x_sample.shape[1]
    keep = min(MAX_WINDOW, s_p)
    pos_prompt = jnp.arange(s_p, dtype=jnp.float32)
    pos_sample = PAST_LEN + jnp.arange(t_s, dtype=jnp.float32)
    shift_zero = jnp.zeros((b_p, d), x_prompt.dtype)
    wkv_zero = jnp.zeros((b_p, N_HEADS_RWKV, HEAD_DIM, HEAD_DIM), jnp.float32)
    y_prompt, y_sample = x_prompt, x_sample
    kp, vp, sp, hp, ks, vs, ss, hs = [], [], [], [], [], [], [], []
    for l in range(DEPTH):
        lw = [w[l] for w in (norm1_g, w_in, q_norm_g, k_norm_g, tmix_mu, decay_w0, decay_up, iclr_a0,
                             iclr_up, gate_up, k_k, k_a, r_k, gn_w, gn_b, w_out, norm2_g,
                             router_group, router_expert, moe_w_gate, moe_w_up, moe_w_down)]
        y_prompt, k, v, wkv, last = hybrid_layer(y_prompt, pos_prompt, shift_zero, wkv_zero,
                                                 None, None, *lw)
        kp.append(k[:, s_p - keep:])
        vp.append(v[:, s_p - keep:])
        sp.append(wkv)
        hp.append(last)
        y_sample, k, v, wkv, last = hybrid_layer(y_sample, pos_sample, state_shift[l], state_wkv[l],
                                                 cache_k_win[l], cache_v_win[l], *lw)
        ks.append(k)
        vs.append(v)
        ss.append(wkv)
        hs.append(last)
    return (y_prompt, y_sample, jnp.stack(kp), jnp.stack(vp), jnp.stack(sp), jnp.stack(hp),
            jnp.stack(ks), jnp.stack(vs), jnp.stack(ss), jnp.stack(hs))
```

```python
import functools

import jax
import jax.numpy as jnp
from jax import lax
from jax.experimental import pallas as pl
from jax.experimental.pallas import tpu as pltpu

D_MODEL = 2048
HEAD_DIM = 64
N_HEADS_ATT = 12
N_HEADS_RWKV = 20
D_ATT = N_HEADS_ATT * HEAD_DIM
D_RWKV = N_HEADS_RWKV * HEAD_DIM
DIL_BRANCHES = ((128, 1), (512, 4), (2048, 16))
MAX_WINDOW = 2048
BAND_BLOCK = 128
ROT_DIM = HEAD_DIM // 4
ROPE_THETA = 500000.0
LORA_DECAY = 96
LORA_ICLR = 96
LORA_GATE = 256
GN_EPS = 64e-5
RMS_EPS = 1e-6
N_GROUPS = 8
EXPERTS_PER_GROUP = 8
N_EXPERTS = N_GROUPS * EXPERTS_PER_GROUP
TOP_K_FINE = 2
D_EXPERT = 1024
PAST_LEN = 16384

LANES = 128
SUBLANES = 8
VMEM_LIMIT_BYTES = 56 * 1024 * 1024

LORA_PAD = LANES
D_RW_PROJ = 3 * D_RWKV + 2 * LORA_PAD + LORA_GATE
N_PAIRS = D_RWKV // LANES
CHUNK = 64
ROUTER_PAD = LANES
MOE_TILE = 256

F32 = jnp.float32
BF16 = jnp.bfloat16
HIGHEST = lax.Precision.HIGHEST


def _cparams(*sem):
    return pltpu.CompilerParams(dimension_semantics=sem, vmem_limit_bytes=VMEM_LIMIT_BYTES)


def _dot(a, b):
    return jnp.dot(a.astype(BF16), b.astype(BF16), preferred_element_type=F32)


def _dot_nt(a, b):
    return lax.dot_general(a.astype(BF16), b.astype(BF16), (((1,), (1,)), ((), ())),
                           preferred_element_type=F32)


def _dot_tn(a, b):
    return lax.dot_general(a.astype(BF16), b.astype(BF16), (((0,), (0,)), ((), ())),
                           preferred_element_type=F32)


def _split2(x):
    hi = x.astype(BF16)
    lo = (x - hi.astype(F32)).astype(BF16)
    return hi, lo


def _split3(x):
    hi = x.astype(BF16)
    r = x - hi.astype(F32)
    mid = r.astype(BF16)
    lo = (r - mid.astype(F32)).astype(BF16)
    return hi, mid, lo


def _dot_exact_rhs(x, e_bf16):
    hi, lo = _split2(x)
    return (jnp.dot(hi, e_bf16, preferred_element_type=F32)
            + jnp.dot(lo, e_bf16, preferred_element_type=F32))


def _head_sum_matrix():
    r = lax.broadcasted_iota(jnp.int32, (LANES, LANES), 0) // HEAD_DIM
    c = lax.broadcasted_iota(jnp.int32, (LANES, LANES), 1) // HEAD_DIM
    return (r == c).astype(BF16)


def _proj_kernel(x_ref, g_ref, w_ref, o_ref, h_ref, *, normalize):
    @pl.when(pl.program_id(1) == 0)
    def _():
        x = x_ref[...]
        if normalize:
            ms = jnp.mean(x * x, axis=-1, keepdims=True)
            x = x * lax.rsqrt(ms + RMS_EPS) * g_ref[...]
        h_ref[...] = x.astype(BF16)

    o_ref[...] = jnp.dot(h_ref[...], w_ref[...], preferred_element_type=F32)


def _norm_proj(x, g, w_bf16, *, normalize, bm, bn, name):
    m, d = x.shape
    n = w_bf16.shape[1]
    return pl.pallas_call(
        functools.partial(_proj_kernel, normalize=normalize),
        grid=(m // bm, n // bn),
        in_specs=[pl.BlockSpec((bm, d), lambda i, j: (i, 0)),
                  pl.BlockSpec((1, d), lambda i, j: (0, 0)),
                  pl.BlockSpec((d, bn), lambda i, j: (0, j))],
        out_specs=pl.BlockSpec((bm, bn), lambda i, j: (i, j)),
        out_shape=jax.ShapeDtypeStruct((m, n), F32),
        scratch_shapes=[pltpu.VMEM((bm, d), BF16)],
        compiler_params=_cparams("parallel", "arbitrary"),
        name=name,
    )(x, g, w_bf16)


def _rms_rows_kernel(x_ref, g_ref, o_ref):
    x = x_ref[...]
    ms = jnp.mean(x * x, axis=-1, keepdims=True)
    o_ref[...] = x * lax.rsqrt(ms + RMS_EPS) * g_ref[...]


def _rms_rows(x, g):
    return pl.pallas_call(
        _rms_rows_kernel,
        out_shape=jax.ShapeDtypeStruct(x.shape, F32),
        name="rms_rows",
    )(x, g)


def _qk_prep_kernel(q_ref, k_ref, cos_ref, sa_ref, sb_ref, qg_ref, kg_ref, qo_ref, ko_ref):
    e = _head_sum_matrix()
    cos = cos_ref[...]
    sa = sa_ref[...]
    sb = sb_ref[...]

    def norm_rope(x, g):
        ss = _dot_exact_rhs(x * x, e)
        y = x * lax.rsqrt(ss * (1.0 / HEAD_DIM) + RMS_EPS) * g
        nxt = pltpu.roll(y, LANES - ROT_DIM // 2, axis=1)
        prv = pltpu.roll(y, ROT_DIM // 2, axis=1)
        return y * cos + nxt * sa + prv * sb

    for c in range(D_ATT // LANES):
        sl = slice(c * LANES, (c + 1) * LANES)
        qo_ref[0, :, sl] = norm_rope(q_ref[0, :, sl], qg_ref[...]) * (HEAD_DIM ** -0.5)
        ko_ref[0, :, sl] = norm_rope(k_ref[0, :, sl], kg_ref[...])


def _rope_tables(pos):
    half = ROT_DIM // 2
    inv_freq = ROPE_THETA ** (-2.0 * jnp.arange(half, dtype=F32) / ROT_DIM)
    ang = pos[:, None] * inv_freq[None, :]
    cos, sin = jnp.cos(ang), jnp.sin(ang)
    t = pos.shape[0]
    one = jnp.ones((t, HEAD_DIM - ROT_DIM), F32)
    zero = jnp.zeros((t, HEAD_DIM - ROT_DIM), F32)
    zh = jnp.zeros((t, half), F32)
    cos_h = jnp.concatenate([cos, cos, one], axis=1)
    sa_h = jnp.concatenate([-sin, zh, zero], axis=1)
    sb_h = jnp.concatenate([zh, sin, zero], axis=1)
    rep = LANES // HEAD_DIM
    return tuple(jnp.tile(z, (1, rep)) for z in (cos_h, sa_h, sb_h))


def _qk_prep(proj_att, pos, q_norm_g, k_norm_g, b, t, bt):
    cos, sa, sb = _rope_tables(pos)
    rep = LANES // HEAD_DIM
    qg = jnp.tile(q_norm_g.reshape(1, HEAD_DIM), (1, rep))
    kg = jnp.tile(k_norm_g.reshape(1, HEAD_DIM), (1, rep))
    tab = pl.BlockSpec((bt, LANES), lambda i, j: (j, 0))
    gsp = pl.BlockSpec((1, LANES), lambda i, j: (0, 0))
    return pl.pallas_call(
        _qk_prep_kernel,
        grid=(b, t // bt),
        in_specs=[pl.BlockSpec((1, bt, D_ATT), lambda i, j: (i, j, 0)),
                  pl.BlockSpec((1, bt, D_ATT), lambda i, j: (i, j, 1)),
                  tab, tab, tab, gsp, gsp],
        out_specs=[pl.BlockSpec((1, bt, D_ATT), lambda i, j: (i, j, 0)),
                   pl.BlockSpec((1, bt, D_ATT), lambda i, j: (i, j, 0))],
        out_shape=[jax.ShapeDtypeStruct((b, t, D_ATT), F32)] * 2,
        compiler_params=_cparams("parallel", "arbitrary"),
        name="qk_prep",
    )(proj_att, proj_att, cos, sa, sb, qg, kg)


def _attn_prompt_kernel(q_ref, k_ref, v_ref, o_ref, acc_ref, m_ref, *, seq):
    blk = BAND_BLOCK
    lane = lax.broadcasted_iota(jnp.int32, (1, LANES), 1)
    n_units = seq // blk
    qi = lax.broadcasted_iota(jnp.int32, (blk, 1), 0)

    for g, (window, dil) in enumerate(DIL_BRANCHES):
        n_back = window // dil
        n_blk = seq // dil // blk
        with_prev = n_blk > 1
        nk = 2 * blk if with_prev else blk
        kj = lax.broadcasted_iota(jnp.int32, (1, nk), 1)
        dist = qi + (blk if with_prev else 0) - kj
        in_band = (dist >= 0) & (dist <= n_back)
        for e in range(LANES // HEAD_DIM):
            own = (lane // HEAD_DIM) == e

            def unit(u, carry, g=g, e=e, dil=dil, own=own, with_prev=with_prev, in_band=in_band, kj=kj):
                r = u % dil
                n = u // dil
                start = r + n * (blk * dil)
                rows = pl.ds(start, blk, stride=dil)
                q = jnp.where(own, q_ref[0, rows, :], 0.0)
                k = k_ref[0, rows, :]
                v = v_ref[0, rows, :]
                valid = in_band
                if with_prev:
                    pstart = jnp.maximum(start - blk * dil, r)
                    prows = pl.ds(pstart, blk, stride=dil)
                    k = jnp.concatenate([k_ref[0, prows, :], k], axis=0)
                    v = jnp.concatenate([v_ref[0, prows, :], v], axis=0)
                    valid = valid & ((kj >= blk) | (n > 0))
                s = _dot_nt(q, k)
                s = jnp.where(valid, s, -jnp.inf)
                m = jnp.max(s, axis=-1, keepdims=True)
                p = jnp.exp(s - m)
                vx = jnp.where(own, v, 1.0)
                acc_ref[g, e, rows, :] = _dot(p, vx)
                m_ref[g, e, rows, :] = m
                return carry

            lax.fori_loop(0, n_units, unit, 0)

    mb = 2 * blk

    def merge(i, carry):
        rows = pl.ds(pl.multiple_of(i * mb, mb), mb)
        out = None
        for e in range(LANES // HEAD_DIM):
            ms = [m_ref[g, e, rows, :] for g in range(len(DIL_BRANCHES))]
            mx = functools.reduce(jnp.maximum, ms)
            tot = None
            for g in range(len(DIL_BRANCHES)):
                term = jnp.exp(ms[g] - mx) * acc_ref[g, e, rows, :]
                tot = term if tot is None else tot + term
            den = pltpu.roll(tot, HEAD_DIM, axis=1)
            oe = tot / den
            out = oe if out is None else jnp.where((lane // HEAD_DIM) == e, oe, out)
        o_ref[0, rows, :] = out.astype(o_ref.dtype)
        return carry

    lax.fori_loop(0, seq // mb, merge, 0)


def _attn_prompt(q, k, proj_att, b, t):
    n_pairs = D_ATT // LANES
    v_off = 2 * D_ATT // LANES
    spec = pl.BlockSpec((1, t, LANES), lambda i, j: (i, 0, j))
    return pl.pallas_call(
        functools.partial(_attn_prompt_kernel, seq=t),
        grid=(b, n_pairs),
        in_specs=[spec, spec, pl.BlockSpec((1, t, LANES), lambda i, j: (i, 0, v_off + j))],
        out_specs=spec,
        out_shape=jax.ShapeDtypeStruct((b, t, D_ATT), BF16),
        scratch_shapes=[pltpu.VMEM((len(DIL_BRANCHES), 2, t, LANES), F32),
                        pltpu.VMEM((len(DIL_BRANCHES), 2, t, 1), F32)],
        compiler_params=_cparams("parallel", "parallel"),
        name="attn_prompt",
    )(q, k, proj_att)


def _attn_sample_kernel(q_ref, kn_ref, vn_ref, ck_ref, cv_ref, o_ref, *, n_ctx, t_new):
    nh = N_HEADS_ATT
    nq = nh * t_new
    q = q_ref[0]
    qt = jnp.concatenate([q] * nh, axis=0)
    rowh = lax.broadcasted_iota(jnp.int32, (nq, D_ATT), 0) // t_new
    laneh = lax.broadcasted_iota(jnp.int32, (nq, D_ATT), 1) // HEAD_DIM
    own = rowh == laneh
    qbd = jnp.where(own, qt, 0.0)

    pad = LANES - t_new
    kn = jnp.concatenate([kn_ref[0], jnp.zeros((pad, D_ATT), F32)], axis=0)
    vn = jnp.concatenate([vn_ref[0], jnp.zeros((pad, D_ATT), F32)], axis=0)

    s_c = _dot_nt(qbd, ck_ref[0])
    s_n = _dot_nt(qbd, kn)

    def multiplicity(d):
        mult = jnp.zeros(d.shape, F32)
        for window, dil in DIL_BRANCHES:
            mult = mult + jnp.where((d >= 0) & (d <= window) & (d % dil == 0), 1.0, 0.0)
        return mult

    qi_c = lax.broadcasted_iota(jnp.int32, (nq, n_ctx), 0) % t_new
    d_c = n_ctx + qi_c - lax.broadcasted_iota(jnp.int32, (nq, n_ctx), 1)
    qi_n = lax.broadcasted_iota(jnp.int32, (nq, LANES), 0) % t_new
    col_n = lax.broadcasted_iota(jnp.int32, (nq, LANES), 1)
    d_n = jnp.where(col_n < t_new, qi_n - col_n, -1)
    mult_c = multiplicity(d_c)
    mult_n = multiplicity(d_n)

    s_c = jnp.where(mult_c > 0, s_c, -jnp.inf)
    s_n = jnp.where(mult_n > 0, s_n, -jnp.inf)
    mx = jnp.maximum(jnp.max(s_c, axis=-1, keepdims=True), jnp.max(s_n, axis=-1, keepdims=True))
    p_c = mult_c * jnp.exp(s_c - mx)
    p_n = mult_n * jnp.exp(s_n - mx)
    den = jnp.sum(p_c, axis=-1, keepdims=True) + jnp.sum(p_n, axis=-1, keepdims=True)
    o_full = (_dot(p_c, cv_ref[0]) + _dot(p_n, vn)) / den
    o_full = jnp.where(own, o_full, 0.0)
    o = o_full[0:t_new]
    for h in range(1, nh):
        o = o + o_full[h * t_new:(h + 1) * t_new]
    o_ref[0] = o.astype(o_ref.dtype)


def _attn_sample(q, k_new, v_new, cache_k, cache_v):
    b, t_new, _ = q.shape
    n_ctx = cache_k.shape[1]
    new = pl.BlockSpec((1, t_new, D_ATT), lambda i: (i, 0, 0))
    ctx = pl.BlockSpec((1, n_ctx, D_ATT), lambda i: (i, 0, 0))
    return pl.pallas_call(
        functools.partial(_attn_sample_kernel, n_ctx=n_ctx, t_new=t_new),
        grid=(b,),
        in_specs=[new, new, new, ctx, ctx],
        out_specs=new,
        out_shape=jax.ShapeDtypeStruct((b, t_new, D_ATT), BF16),
        compiler_params=_cparams("parallel"),
        name="attn_sample",
    )(q, k_new, v_new, cache_k, cache_v)


def _rwkv_prep_kernel(p_ref, sh_ref, mu_ref, w0_ref, dup_ref, a0_ref, iup_ref, gup_ref, kk_ref, ka_ref,
                      r_o, k_o, v_o, kkn_o, beta_o, logd_o, g_o, carry_ref):
    c1, c2, c3 = D_RWKV, 2 * D_RWKV, 3 * D_RWKV
    c4, c5 = c3 + LORA_PAD, c3 + 2 * LORA_PAD

    @pl.when(pl.program_id(1) == 0)
    def _():
        carry_ref[...] = sh_ref[0]

    cur = p_ref[0]
    bt = cur.shape[0]
    prev = pltpu.roll(cur, 1, axis=0)
    row = lax.broadcasted_iota(jnp.int32, (bt, 1), 0)
    prev = jnp.where(row == 0, carry_ref[...], prev)
    carry_ref[...] = cur[bt - 1:bt]
    xm = cur + (prev - cur) * mu_ref[...]

    r, k, v = xm[:, :c1], xm[:, c1:c2], xm[:, c2:c3]
    wd, ad, gd = xm[:, c3:c4], xm[:, c4:c5], xm[:, c5:]

    z = -(w0_ref[...] + jnp.dot(jnp.tanh(wd), dup_ref[...], precision=HIGHEST, preferred_element_type=F32))
    softplus = jnp.maximum(z, 0.0) + jnp.log(1.0 + jnp.exp(-jnp.abs(z)))
    w = -softplus - 0.5
    logd_o[0] = -jnp.exp(w)
    a = jax.nn.sigmoid(a0_ref[...] + jnp.dot(ad, iup_ref[...], precision=HIGHEST,
                                             preferred_element_type=F32))
    g_o[0] = jnp.dot(jax.nn.sigmoid(gd), gup_ref[...], precision=HIGHEST, preferred_element_type=F32)

    kk = k * kk_ref[...]
    e = _head_sum_matrix()
    for c in range(N_PAIRS):
        sl = slice(c * LANES, (c + 1) * LANES)
        kc = kk[:, sl]
        ss = _dot_exact_rhs(kc * kc, e)
        kn = kc * lax.rsqrt(jnp.maximum(ss, 1e-24))
        kkn_o[0, :, sl] = kn
        beta_o[0, :, sl] = kn * a[:, sl]
    r_o[0] = r
    k_o[0] = k * (1.0 + (a - 1.0) * ka_ref[...])
    v_o[0] = v


def _rwkv_prep(proj_rw, shift_proj, prm, b, t, bt):
    blk = pl.BlockSpec((1, bt, D_RW_PROJ), lambda i, j: (i, j, 0))
    full = lambda a: pl.BlockSpec(a.shape, lambda i, j: (0,) * a.ndim)
    out = pl.BlockSpec((1, bt, D_RWKV), lambda i, j: (i, j, 0))
    args = (prm["mu"], prm["w0"], prm["decay_up"], prm["a0"], prm["iclr_up"], prm["gate_up"],
            prm["k_k"], prm["k_a"])
    return pl.pallas_call(
        _rwkv_prep_kernel,
        grid=(b, t // bt),
        in_specs=[blk, pl.BlockSpec((1, 1, D_RW_PROJ), lambda i, j: (i, 0, 0))] + [full(a) for a in args],
        out_specs=[out] * 7,
        out_shape=[jax.ShapeDtypeStruct((b, t, D_RWKV), F32)] * 7,
        scratch_shapes=[pltpu.VMEM((1, D_RW_PROJ), F32)],
        compiler_params=_cparams("parallel", "arbitrary"),
        name="rwkv_prep",
    )(proj_rw, shift_proj, *args)


def _rwkv_chunk_kernel(r_ref, k_ref, v_ref, kkn_ref, beta_ref, logd_ref, g_ref, s0_ref,
                       rk_ref, gnw_ref, gnb_ref, o_ref, sout_ref, s_ref):
    c = CHUNK
    ci = pl.program_id(1)

    @pl.when(ci == 0)
    def _():
        s_ref[...] = s0_ref[0]

    tri = (lax.broadcasted_iota(jnp.int32, (c, c), 0) >= lax.broadcasted_iota(jnp.int32, (c, c), 1)).astype(BF16)
    logd = logd_ref[0]
    l1, l2, l3 = _split3(logd)
    cum = (jnp.dot(tri, l1, preferred_element_type=F32) + jnp.dot(tri, l2, preferred_element_type=F32)
           + jnp.dot(tri, l3, preferred_element_type=F32))
    total = cum[c - 1:c]
    e_in = jnp.exp(cum)
    e_ex = jnp.exp(cum - logd)
    e_neg = jnp.exp(-cum)
    e_end = jnp.exp(total - cum)
    p_end = jnp.exp(total)

    r = r_ref[0]
    k = k_ref[0]
    v = v_ref[0]
    beta = beta_ref[0]
    alpha_t = -kkn_ref[0] * e_ex
    r_t = r * e_in
    beta_t = beta * e_neg
    k_t = k * e_neg
    beta_e = beta * e_end
    k_e = k * e_end

    lane = lax.broadcasted_iota(jnp.int32, (1, LANES), 1)
    lo = lane < HEAD_DIM
    rr = lax.broadcasted_iota(jnp.int32, (2 * c, 2 * c), 0)
    cc = lax.broadcasted_iota(jnp.int32, (2 * c, 2 * c), 1)
    same_head = (rr // c) == (cc // c)
    strict = same_head & (cc < rr)
    incl = same_head & (cc <= rr)

    def expand(x):
        return jnp.concatenate([jnp.where(lo, x, 0.0), jnp.where(lo, 0.0, x)], axis=0)

    def twice(x):
        return jnp.concatenate([x, x], axis=0)

    outs = []
    for p in range(N_PAIRS):
        sl = slice(p * LANES, (p + 1) * LANES)
        s_prev = s_ref[p]
        a_x = expand(alpha_t[:, sl])
        r_x = expand(r_t[:, sl])
        v_x = expand(v[:, sl])
        lx = jnp.concatenate([a_x, r_x], axis=0)
        rx = jnp.concatenate([twice(beta_t[:, sl]), twice(k_t[:, sl])], axis=0)
        gmat = _dot_nt(lx, rx)
        n_ab = jnp.where(strict, gmat[:2 * c, :2 * c], 0.0)
        a_ak = jnp.where(strict, gmat[:2 * c, 2 * c:], 0.0)
        a_rb = jnp.where(incl, gmat[2 * c:, :2 * c], 0.0)
        a_rk = jnp.where(incl, gmat[2 * c:, 2 * c:], 0.0)
        ls = _dot_nt(lx, s_prev)
        u = ls[:2 * c] + _dot(a_ak, v_x)
        npow = n_ab
        steps = c.bit_length() - 1
        for it in range(steps):
            u = u + _dot(npow, u)
            if it + 1 < steps:
                npow = _dot(npow, npow)
        o_x = ls[2 * c:] + _dot(jnp.concatenate([a_rb, a_rk], axis=1), jnp.concatenate([u, v_x], axis=0))
        outs.append(o_x[:c] + o_x[c:])
        upd = _dot_tn(jnp.concatenate([u, v_x], axis=0),
                      jnp.concatenate([expand(beta_e[:, sl]), expand(k_e[:, sl])], axis=0))
        s_ref[p] = s_prev * p_end[:, sl] + upd

    e = _head_sum_matrix()
    o_all = jnp.concatenate(outs, axis=0)

    def stack(x):
        return jnp.concatenate([x[:, p * LANES:(p + 1) * LANES] for p in range(N_PAIRS)], axis=0)

    def stack_param(ref):
        x = ref[...]
        return jnp.concatenate([jnp.broadcast_to(x[:, p * LANES:(p + 1) * LANES], (c, LANES))
                                for p in range(N_PAIRS)], axis=0)

    mean = _dot_exact_rhs(o_all, e) * (1.0 / HEAD_DIM)
    d = o_all - mean
    var = _dot_exact_rhs(d * d, e) * (1.0 / HEAD_DIM)
    y = d * lax.rsqrt(var + GN_EPS) * stack_param(gnw_ref) + stack_param(gnb_ref)
    bonus = _dot_exact_rhs(stack(r * k) * stack_param(rk_ref), e)
    y = (y + bonus * stack(v)) * stack(g_ref[0])
    for p in range(N_PAIRS):
        o_ref[0, :, p * LANES:(p + 1) * LANES] = y[p * c:(p + 1) * c].astype(o_ref.dtype)

    @pl.when(ci == pl.num_programs(1) - 1)
    def _():
        sout_ref[0] = s_ref[...]


def _rwkv_chunk(r, k, v, kkn, beta, logd, g, s0, prm, b, t):
    blk = pl.BlockSpec((1, CHUNK, D_RWKV), lambda i, j: (i, j, 0))
    st = pl.BlockSpec((1, N_PAIRS, LANES, LANES), lambda i, j: (i, 0, 0, 0))
    par = pl.BlockSpec((1, D_RWKV), lambda i, j: (0, 0))
    return pl.pallas_call(
        _rwkv_chunk_kernel,
        grid=(b, t // CHUNK),
        in_specs=[blk] * 7 + [st, par, par, par],
        out_specs=[blk, st],
        out_shape=[jax.ShapeDtypeStruct((b, t, D_RWKV), BF16),
                   jax.ShapeDtypeStruct((b, N_PAIRS, LANES, LANES), F32)],
        scratch_shapes=[pltpu.VMEM((N_PAIRS, LANES, LANES), F32)],
        compiler_params=_cparams("parallel", "arbitrary"),
        name="rwkv_chunk",
    )(r, k, v, kkn, beta, logd, g, s0, prm["r_k"], prm["gn_w"], prm["gn_b"])


def _state_to_pairs(wkv):
    b = wkv.shape[0]
    w = wkv.reshape(b, N_PAIRS, 2, HEAD_DIM, HEAD_DIM)
    z = jnp.zeros((b, N_PAIRS, HEAD_DIM, HEAD_DIM), wkv.dtype)
    top = jnp.concatenate([w[:, :, 0], z], axis=-1)
    bot = jnp.concatenate([z, w[:, :, 1]], axis=-1)
    return jnp.concatenate([top, bot], axis=-2)


def _pairs_to_state(s):
    b = s.shape[0]
    h0 = s[:, :, :HEAD_DIM, :HEAD_DIM]
    h1 = s[:, :, HEAD_DIM:, HEAD_DIM:]
    return jnp.stack([h0, h1], axis=2).reshape(b, N_HEADS_RWKV, HEAD_DIM, HEAD_DIM)


def _out_router_kernel(x_ref, a_ref, r_ref, wa_ref, wr_ref, g2_ref, rt1_ref, rt2_ref,
                       x2_ref, h2_ref, lg_ref):
    x2 = (x_ref[...] + jnp.dot(a_ref[...], wa_ref[...], preferred_element_type=F32)
          + jnp.dot(r_ref[...], wr_ref[...], preferred_element_type=F32))
    x2_ref[...] = x2
    ms = jnp.mean(x2 * x2, axis=-1, keepdims=True)
    h2 = x2 * lax.rsqrt(ms + RMS_EPS) * g2_ref[...]
    hi, lo = _split2(h2)
    h2_ref[...] = hi
    t12 = jnp.dot(hi, rt1_ref[...], preferred_element_type=F32)
    t3 = jnp.dot(lo, rt2_ref[...], preferred_element_type=F32)
    lg_ref[...] = t12[:, :ROUTER_PAD] + t12[:, ROUTER_PAD:] + t3


def _out_router(x, o_att, o_rwkv, wa, wr, g2, rt1, rt2, bm):
    m, d = x.shape
    row = lambda w: pl.BlockSpec((bm, w), lambda i: (i, 0))
    full = lambda a: pl.BlockSpec(a.shape, lambda i: (0,) * a.ndim)
    return pl.pallas_call(
        _out_router_kernel,
        grid=(m // bm,),
        in_specs=[row(d), row(D_ATT), row(D_RWKV), full(wa), full(wr), full(g2), full(rt1), full(rt2)],
        out_specs=[row(d), row(d), row(ROUTER_PAD)],
        out_shape=[jax.ShapeDtypeStruct((m, d), F32), jax.ShapeDtypeStruct((m, d), BF16),
                   jax.ShapeDtypeStruct((m, ROUTER_PAD), F32)],
        compiler_params=_cparams("parallel"),
        name="out_router",
    )(x, o_att, o_rwkv, wa, wr, g2, rt1, rt2)


def _moe_kernel(te_ref, tx_ref, nv_ref, x_ref, wg_ref, wu_ref, wd_ref, y_ref):
    @pl.when(pl.program_id(0) < nv_ref[0])
    def _():
        x = x_ref[...]
        gt = jnp.dot(x, wg_ref[0], preferred_element_type=F32)
        up = jnp.dot(x, wu_ref[0], preferred_element_type=F32)
        hmid = (gt * jax.nn.sigmoid(gt) * up).astype(BF16)
        y_ref[...] = jnp.dot(hmid, wd_ref[0], preferred_element_type=F32)


def _moe_ffn(xs, tile_expert, tile_index, n_valid, wg, wu, wd):
    rows, d = xs.shape
    n_tiles = rows // MOE_TILE
    grid_spec = pltpu.PrefetchScalarGridSpec(
        num_scalar_prefetch=3,
        grid=(n_tiles,),
        in_specs=[pl.BlockSpec((MOE_TILE, d), lambda i, te, tx, nv: (tx[i], 0)),
                  pl.BlockSpec((1, d, D_EXPERT), lambda i, te, tx, nv: (te[i], 0, 0)),
                  pl.BlockSpec((1, d, D_EXPERT), lambda i, te, tx, nv: (te[i], 0, 0)),
                  pl.BlockSpec((1, D_EXPERT, d), lambda i, te, tx, nv: (te[i], 0, 0))],
        out_specs=pl.BlockSpec((MOE_TILE, d), lambda i, te, tx, nv: (tx[i], 0)),
    )
    return pl.pallas_call(
        _moe_kernel,
        grid_spec=grid_spec,
        out_shape=jax.ShapeDtypeStruct((rows, d), F32),
        compiler_params=_cparams("arbitrary"),
        name="moe_ffn",
    )(tile_expert, tile_index, n_valid, xs, wg, wu, wd)


def _route(logits):
    lg = logits[:, :N_GROUPS]
    pg = jax.nn.softmax(lg, axis=-1)
    grp = jnp.argmax(lg, axis=-1)
    p_grp = jnp.take_along_axis(pg, grp[:, None], axis=-1)
    le = logits[:, N_GROUPS:N_GROUPS + N_EXPERTS].reshape(-1, N_GROUPS, EXPERTS_PER_GROUP)
    le = jnp.take_along_axis(le, grp[:, None, None], axis=1)[:, 0]
    top_p, top_i = lax.top_k(jax.nn.softmax(le, axis=-1), TOP_K_FINE)
    gate = p_grp * top_p / jnp.sum(top_p, axis=-1, keepdims=True)
    eid = grp[:, None] * EXPERTS_PER_GROUP + top_i
    return eid.astype(jnp.int32), gate


def _moe(h2, logits, wg, wu, wd):
    n, d = h2.shape
    eid, gate = _route(logits)
    m = n * TOP_K_FINE
    eid_f = eid.reshape(-1)
    tok = jnp.repeat(jnp.arange(n, dtype=jnp.int32), TOP_K_FINE)
    order = jnp.argsort(eid_f)
    e_s, tok_s = eid_f[order], tok[order]
    counts = jnp.zeros((N_EXPERTS,), jnp.int32).at[eid_f].add(1)
    padded = (counts + MOE_TILE - 1) // MOE_TILE * MOE_TILE
    pend = jnp.cumsum(padded)
    pstart = pend - padded
    start = jnp.cumsum(counts) - counts
    dest = pstart[e_s] + jnp.arange(m, dtype=jnp.int32) - start[e_s]
    n_tiles = -(-(m + N_EXPERTS * (MOE_TILE - 1)) // MOE_TILE)
    cap = n_tiles * MOE_TILE
    buf_tok = jnp.zeros((cap,), jnp.int32).at[dest].set(tok_s)
    xs = h2[buf_tok]
    n_valid = (pend[-1] // MOE_TILE).astype(jnp.int32)
    tiles = jnp.arange(n_tiles, dtype=jnp.int32)
    tile_index = jnp.minimum(tiles, n_valid - 1)
    tile_expert = jnp.minimum(jnp.searchsorted(pend, tile_index * MOE_TILE, side='right'),
                              N_EXPERTS - 1).astype(jnp.int32)
    ys = _moe_ffn(xs, tile_expert, tile_index, n_valid.reshape(1), wg, wu, wd)
    pos = jnp.zeros((m,), jnp.int32).at[order].set(dest).reshape(n, TOP_K_FINE)
    return ys[pos[:, 0]] * gate[:, 0:1] + ys[pos[:, 1]] * gate[:, 1:2]


def _pad_cols(w, width):
    return jnp.pad(w, ((0, 0), (0, width - w.shape[1])))


def _pad_rows(w, height):
    return jnp.pad(w, ((0, height - w.shape[0]), (0, 0)))


def _prepare_params(w_in, tmix_mu, decay_w0, decay_up, iclr_a0, iclr_up, gate_up, k_k, k_a, r_k,
                    gn_w, gn_b, w_out, router_group, router_expert):
    c_att = 3 * D_ATT
    c3 = c_att + 3 * D_RWKV
    c4, c5 = c3 + LORA_DECAY, c3 + LORA_DECAY + LORA_ICLR
    w_rw = jnp.concatenate([w_in[:, c_att:c3], _pad_cols(w_in[:, c3:c4], LORA_PAD),
                            _pad_cols(w_in[:, c4:c5], LORA_PAD), w_in[:, c5:]], axis=1)
    mu = tmix_mu.reshape(1, -1)
    o3 = 3 * D_RWKV
    mu = jnp.concatenate([mu[:, :o3], _pad_cols(mu[:, o3:o3 + LORA_DECAY], LORA_PAD),
                          _pad_cols(mu[:, o3 + LORA_DECAY:o3 + LORA_DECAY + LORA_ICLR], LORA_PAD),
                          mu[:, o3 + LORA_DECAY + LORA_ICLR:]], axis=1)
    router = _pad_cols(jnp.concatenate([router_group, router_expert], axis=1), ROUTER_PAD)
    rt_hi = router.astype(BF16)
    rt_lo = (router - rt_hi.astype(F32)).astype(BF16)
    row = lambda z: z.reshape(1, -1)
    return dict(
        w_att=w_in[:, :c_att].astype(BF16), w_rw=w_rw.astype(BF16), mu=mu,
        w0=row(decay_w0), decay_up=_pad_rows(decay_up, LORA_PAD), a0=row(iclr_a0),
        iclr_up=_pad_rows(iclr_up, LORA_PAD), gate_up=gate_up, k_k=row(k_k), k_a=row(k_a),
        r_k=row(r_k), gn_w=row(gn_w), gn_b=row(gn_b),
        wo_att=w_out[:D_ATT].astype(BF16), wo_rw=w_out[D_ATT:].astype(BF16),
        rt1=jnp.concatenate([rt_hi, rt_lo], axis=1), rt2=rt_hi)


def _mixers(x, pos, shift_row, wkv0, k_ctx, v_ctx, prm, norm1_g, q_norm_g, k_norm_g, bm, bt):
    b, t, d = x.shape
    rows = b * t
    x2d = x.reshape(rows, d)
    g1 = norm1_g.reshape(1, d)
    proj_att = _norm_proj(x2d, g1, prm["w_att"], normalize=True, bm=bm, bn=prm["w_att"].shape[1] // 2,
                          name="proj_att").reshape(b, t, 3 * D_ATT)
    proj_rw = _norm_proj(x2d, g1, prm["w_rw"], normalize=True, bm=bm, bn=D_RW_PROJ // 2,
                         name="proj_rw").reshape(b, t, D_RW_PROJ)
    if shift_row is None:
        shift_proj = jnp.zeros((b, 1, D_RW_PROJ), F32)
    else:
        shift_proj = _norm_proj(shift_row, g1, prm["w_rw"], normalize=False, bm=b, bn=D_RW_PROJ // 2,
                                name="proj_shift").reshape(b, 1, D_RW_PROJ)

    q, k = _qk_prep(proj_att, pos, q_norm_g, k_norm_g, b, t, bt)
    v = proj_att[:, :, 2 * D_ATT:]
    if k_ctx is None:
        o_att = _attn_prompt(q, k, proj_att, b, t)
    else:
        o_att = _attn_sample(q, k, v, k_ctx, v_ctx)

    r, kr, vr, kkn, beta, logd, g = _rwkv_prep(proj_rw, shift_proj, prm, b, t, bt)
    s0 = _state_to_pairs(wkv0)
    tp = -(-t // CHUNK) * CHUNK
    if tp != t:
        padt = lambda z: jnp.pad(z, ((0, 0), (0, tp - t), (0, 0)))
        r, kr, vr, kkn, beta, logd, g = (padt(z) for z in (r, kr, vr, kkn, beta, logd, g))
    o_rwkv, s_new = _rwkv_chunk(r, kr, vr, kkn, beta, logd, g, s0, prm, b, tp)
    o_rwkv = o_rwkv[:, :t]
    return (o_att.reshape(rows, D_ATT), o_rwkv.reshape(rows, D_RWKV), k, v, _pairs_to_state(s_new))


def kernel(x_prompt, x_sample, cache_k_win, cache_v_win, state_wkv, state_shift, norm1_g, w_in, q_norm_g,
           k_norm_g, tmix_mu, decay_w0, decay_up, iclr_a0, iclr_up, gate_up, k_k, k_a, r_k, gn_w, gn_b,
           w_out, norm2_g, router_group, router_expert, moe_w_gate, moe_w_up, moe_w_down):
    depth = w_in.shape[0]
    assert depth == 1
    l = 0
    b_p, s_p, d = x_prompt.shape
    b_s, t_s, _ = x_sample.shape
    n_ctx = cache_k_win.shape[2]
    prm = _prepare_params(w_in[l], tmix_mu[l], decay_w0[l], decay_up[l], iclr_a0[l], iclr_up[l],
                          gate_up[l], k_k[l], k_a[l], r_k[l], gn_w[l], gn_b[l], w_out[l],
                          router_group[l], router_expert[l])
    g2 = norm2_g[l].reshape(1, d)

    pos_p = jnp.arange(s_p, dtype=F32)
    pos_s = PAST_LEN + jnp.arange(t_s, dtype=F32)
    wkv_zero = jnp.zeros((b_p, N_HEADS_RWKV, HEAD_DIM, HEAD_DIM), F32)

    oa_p, or_p, k_p, v_p, wkv_p = _mixers(x_prompt, pos_p, None, wkv_zero, None, None, prm,
                                          norm1_g[l], q_norm_g[l], k_norm_g[l], bm=512, bt=256)
    ck = cache_k_win[l].reshape(b_s, n_ctx, D_ATT)
    cv = cache_v_win[l].reshape(b_s, n_ctx, D_ATT)
    oa_s, or_s, k_s, v_s, wkv_s = _mixers(x_sample, pos_s, state_shift[l], state_wkv[l], ck, cv, prm,
                                          norm1_g[l], q_norm_g[l], k_norm_g[l], bm=b_s * t_s, bt=t_s)

    n_p, n_s = b_p * s_p, b_s * t_s
    x2_p, h2_p, lg_p = _out_router(x_prompt.reshape(n_p, d), oa_p, or_p, prm["wo_att"], prm["wo_rw"], g2,
                                   prm["rt1"], prm["rt2"], bm=512)
    x2_s, h2_s, lg_s = _out_router(x_sample.reshape(n_s, d), oa_s, or_s, prm["wo_att"], prm["wo_rw"], g2,
                                   prm["rt1"], prm["rt2"], bm=n_s)

    h2 = jnp.concatenate([h2_p, h2_s], axis=0)
    lg = jnp.concatenate([lg_p, lg_s], axis=0)
    moe = _moe(h2, lg, moe_w_gate[l].astype(BF16), moe_w_up[l].astype(BF16), moe_w_down[l].astype(BF16))
    y_p = (x2_p + moe[:n_p]).reshape(b_p, s_p, d)
    y_s = (x2_s + moe[n_p:]).reshape(b_s, t_s, d)

    last = jnp.concatenate([x_prompt[:, -1], x_sample[:, -1]], axis=0)
    shift = _rms_rows(last, norm1_g[l].reshape(1, d))

    keep = min(MAX_WINDOW, s_p)
    heads = lambda z, b, t: z.reshape(1, b, t, N_HEADS_ATT, HEAD_DIM)
    return (y_p, y_s,
            heads(k_p[:, s_p - keep:], b_p, keep), heads(v_p[:, s_p - keep:], b_p, keep),
            wkv_p[None], shift[:b_p][None],
            heads(k_s, b_s, t_s), heads(v_s, b_s, t_s), wkv_s[None], shift[b_p:][None])
```

```python
import functools

import jax
import jax.numpy as jnp
from jax import lax
from jax.experimental import pallas as pl
from jax.experimental.pallas import tpu as pltpu

D_MODEL = 2048
HEAD_DIM = 64
N_HEADS_ATT = 12
N_HEADS_RWKV = 20
D_ATT = N_HEADS_ATT * HEAD_DIM
D_RWKV = N_HEADS_RWKV * HEAD_DIM
DIL_BRANCHES = ((128, 1), (512, 4), (2048, 16))
MAX_WINDOW = 2048
BAND_BLOCK = 128
ROT_DIM = HEAD_DIM // 4
ROPE_THETA = 500000.0
LORA_DECAY = 96
LORA_ICLR = 96
LORA_GATE = 256
GN_EPS = 64e-5
RMS_EPS = 1e-6
N_GROUPS = 8
EXPERTS_PER_GROUP = 8
N_EXPERTS = N_GROUPS * EXPERTS_PER_GROUP
TOP_K_FINE = 2
D_EXPERT = 1024
PAST_LEN = 16384

LANES = 128
SUBLANES = 8
VMEM_LIMIT_BYTES = 56 * 1024 * 1024

LORA_PAD = LANES
D_RW_PROJ = 3 * D_RWKV + 2 * LORA_PAD + LORA_GATE
N_PAIRS = D_RWKV // LANES
CHUNK = 64
ROUTER_PAD = LANES
MOE_TILE = 256

F32 = jnp.float32
BF16 = jnp.bfloat16
HIGHEST = lax.Precision.HIGHEST


def _cparams(*sem):
    return pltpu.CompilerParams(dimension_semantics=("arbitrary",) * len(sem),
                                vmem_limit_bytes=VMEM_LIMIT_BYTES)


def _dot(a, b):
    return jnp.dot(a.astype(BF16), b.astype(BF16), preferred_element_type=F32)


def _dot_nt(a, b):
    return lax.dot_general(a.astype(BF16), b.astype(BF16), (((1,), (1,)), ((), ())),
                           preferred_element_type=F32)


def _dot_tn(a, b):
    return lax.dot_general(a.astype(BF16), b.astype(BF16), (((0,), (0,)), ((), ())),
                           preferred_element_type=F32)


def _split2(x):
    hi = x.astype(BF16)
    lo = (x - hi.astype(F32)).astype(BF16)
    return hi, lo


def _split3(x):
    hi = x.astype(BF16)
    r = x - hi.astype(F32)
    mid = r.astype(BF16)
    lo = (r - mid.astype(F32)).astype(BF16)
    return hi, mid, lo


def _dot_exact_rhs(x, e_bf16):
    hi, lo = _split2(x)
    return (jnp.dot(hi, e_bf16, preferred_element_type=F32)
            + jnp.dot(lo, e_bf16, preferred_element_type=F32))


def _head_sum_matrix():
    r = lax.broadcasted_iota(jnp.int32, (LANES, LANES), 0) // HEAD_DIM
    c = lax.broadcasted_iota(jnp.int32, (LANES, LANES), 1) // HEAD_DIM
    return (r == c).astype(BF16)


def _proj_kernel(x_ref, g_ref, w_ref, o_ref, h_ref, *, normalize):
    @pl.when(pl.program_id(1) == 0)
    def _():
        x = x_ref[...]
        if normalize:
            ms = jnp.mean(x * x, axis=-1, keepdims=True)
            x = x * lax.rsqrt(ms + RMS_EPS) * g_ref[...]
        h_ref[...] = x.astype(BF16)

    o_ref[...] = jnp.dot(h_ref[...], w_ref[...], preferred_element_type=F32)


def _norm_proj(x, g, w_bf16, *, normalize, bm, bn, name):
    m, d = x.shape
    n = w_bf16.shape[1]
    return pl.pallas_call(
        functools.partial(_proj_kernel, normalize=normalize),
        grid=(m // bm, n // bn),
        in_specs=[pl.BlockSpec((bm, d), lambda i, j: (i, 0)),
                  pl.BlockSpec((1, d), lambda i, j: (0, 0)),
                  pl.BlockSpec((d, bn), lambda i, j: (0, j))],
        out_specs=pl.BlockSpec((bm, bn), lambda i, j: (i, j)),
        out_shape=jax.ShapeDtypeStruct((m, n), F32),
        scratch_shapes=[pltpu.VMEM((bm, d), BF16)],
        compiler_params=_cparams("parallel", "arbitrary"),
        name=name,
    )(x, g, w_bf16)


def _rms_rows_kernel(x_ref, g_ref, o_ref):
    x = x_ref[...]
    ms = jnp.mean(x * x, axis=-1, keepdims=True)
    o_ref[...] = x * lax.rsqrt(ms + RMS_EPS) * g_ref[...]


def _rms_rows(x, g):
    return pl.pallas_call(
        _rms_rows_kernel,
        out_shape=jax.ShapeDtypeStruct(x.shape, F32),
        name="rms_rows",
    )(x, g)


def _qk_prep_kernel(q_ref, k_ref, cos_ref, sa_ref, sb_ref, qg_ref, kg_ref, qo_ref, ko_ref):
    e = _head_sum_matrix()
    cos = cos_ref[...]
    sa = sa_ref[...]
    sb = sb_ref[...]

    def norm_rope(x, g):
        ss = _dot_exact_rhs(x * x, e)
        y = x * lax.rsqrt(ss * (1.0 / HEAD_DIM) + RMS_EPS) * g
        nxt = pltpu.roll(y, LANES - ROT_DIM // 2, axis=1)
        prv = pltpu.roll(y, ROT_DIM // 2, axis=1)
        return y * cos + nxt * sa + prv * sb

    for c in range(D_ATT // LANES):
        sl = slice(c * LANES, (c + 1) * LANES)
        qo_ref[0, :, sl] = norm_rope(q_ref[0, :, sl], qg_ref[...]) * (HEAD_DIM ** -0.5)
        ko_ref[0, :, sl] = norm_rope(k_ref[0, :, sl], kg_ref[...])


def _rope_tables(pos):
    half = ROT_DIM // 2
    inv_freq = ROPE_THETA ** (-2.0 * jnp.arange(half, dtype=F32) / ROT_DIM)
    ang = pos[:, None] * inv_freq[None, :]
    cos, sin = jnp.cos(ang), jnp.sin(ang)
    t = pos.shape[0]
    one = jnp.ones((t, HEAD_DIM - ROT_DIM), F32)
    zero = jnp.zeros((t, HEAD_DIM - ROT_DIM), F32)
    zh = jnp.zeros((t, half), F32)
    cos_h = jnp.concatenate([cos, cos, one], axis=1)
    sa_h = jnp.concatenate([-sin, zh, zero], axis=1)
    sb_h = jnp.concatenate([zh, sin, zero], axis=1)
    rep = LANES // HEAD_DIM
    return tuple(jnp.tile(z, (1, rep)) for z in (cos_h, sa_h, sb_h))


def _qk_prep(proj_att, pos, q_norm_g, k_norm_g, b, t, bt):
    cos, sa, sb = _rope_tables(pos)
    rep = LANES // HEAD_DIM
    qg = jnp.tile(q_norm_g.reshape(1, HEAD_DIM), (1, rep))
    kg = jnp.tile(k_norm_g.reshape(1, HEAD_DIM), (1, rep))
    tab = pl.BlockSpec((bt, LANES), lambda i, j: (j, 0))
    gsp = pl.BlockSpec((1, LANES), lambda i, j: (0, 0))
    return pl.pallas_call(
        _qk_prep_kernel,
        grid=(b, t // bt),
        in_specs=[pl.BlockSpec((1, bt, D_ATT), lambda i, j: (i, j, 0)),
                  pl.BlockSpec((1, bt, D_ATT), lambda i, j: (i, j, 1)),
                  tab, tab, tab, gsp, gsp],
        out_specs=[pl.BlockSpec((1, bt, D_ATT), lambda i, j: (i, j, 0)),
                   pl.BlockSpec((1, bt, D_ATT), lambda i, j: (i, j, 0))],
        out_shape=[jax.ShapeDtypeStruct((b, t, D_ATT), F32)] * 2,
        compiler_params=_cparams("parallel", "arbitrary"),
        name="qk_prep",
    )(proj_att, proj_att, cos, sa, sb, qg, kg)


def _attn_prompt_kernel(q_ref, k_ref, v_ref, o_ref, acc_ref, m_ref, *, seq):
    blk = BAND_BLOCK
    lane = lax.broadcasted_iota(jnp.int32, (1, LANES), 1)
    lo = lane < HEAD_DIM
    n_units = seq // blk
    qi = lax.broadcasted_iota(jnp.int32, (2 * blk, 1), 0) % blk

    for g, (window, dil) in enumerate(DIL_BRANCHES):
        n_back = window // dil
        n_blk = seq // dil // blk
        with_prev = n_blk > 1
        nk = 2 * blk if with_prev else blk
        kj = lax.broadcasted_iota(jnp.int32, (1, nk), 1)
        dist = qi + (blk if with_prev else 0) - kj
        in_band = (dist >= 0) & (dist <= n_back)

        def unit(u, carry, g=g, dil=dil, with_prev=with_prev, in_band=in_band, kj=kj):
            r = u % dil
            n = u // dil
            start = r + n * (blk * dil)
            rows = pl.ds(start, blk, stride=dil)
            q = q_ref[0, rows, :]
            k = k_ref[0, rows, :]
            v = v_ref[0, rows, :]
            valid = in_band
            if with_prev:
                pstart = jnp.maximum(start - blk * dil, r)
                prows = pl.ds(pstart, blk, stride=dil)
                k = jnp.concatenate([k_ref[0, prows, :], k], axis=0)
                v = jnp.concatenate([v_ref[0, prows, :], v], axis=0)
                valid = valid & ((kj >= blk) | (n > 0))
            qq = jnp.concatenate([jnp.where(lo, q, 0.0), jnp.where(lo, 0.0, q)], axis=0)
            s = _dot_nt(qq, k)
            s = jnp.where(valid, s, -jnp.inf)
            m = jnp.max(s, axis=-1, keepdims=True)
            p = jnp.exp(s - m).astype(BF16)
            acc_ref[g, 0, rows, :] = _dot(p[:blk], jnp.where(lo, v, 1.0))
            acc_ref[g, 1, rows, :] = _dot(p[blk:], jnp.where(lo, 1.0, v))
            m_ref[g, 0, rows, :] = m[:blk]
            m_ref[g, 1, rows, :] = m[blk:]
            return carry

        lax.fori_loop(0, n_units, unit, 0, unroll=2)

    mb = 2 * blk

    def merge(i, carry):
        rows = pl.ds(pl.multiple_of(i * mb, mb), mb)
        out = None
        for e in range(LANES // HEAD_DIM):
            ms = [m_ref[g, e, rows, :] for g in range(len(DIL_BRANCHES))]
            mx = functools.reduce(jnp.maximum, ms)
            tot = None
            for g in range(len(DIL_BRANCHES)):
                term = jnp.exp(ms[g] - mx) * acc_ref[g, e, rows, :]
                tot = term if tot is None else tot + term
            den = pltpu.roll(tot, HEAD_DIM, axis=1)
            oe = tot / den
            out = oe if out is None else jnp.where((lane // HEAD_DIM) == e, oe, out)
        o_ref[0, rows, :] = out.astype(o_ref.dtype)
        return carry

    lax.fori_loop(0, seq // mb, merge, 0, unroll=2)


def _attn_prompt(q, k, proj_att, b, t):
    n_pairs = D_ATT // LANES
    v_off = 2 * D_ATT // LANES
    spec = pl.BlockSpec((1, t, LANES), lambda i, j: (i, 0, j))
    return pl.pallas_call(
        functools.partial(_attn_prompt_kernel, seq=t),
        grid=(b, n_pairs),
        in_specs=[spec, spec, pl.BlockSpec((1, t, LANES), lambda i, j: (i, 0, v_off + j))],
        out_specs=spec,
        out_shape=jax.ShapeDtypeStruct((b, t, D_ATT), BF16),
        scratch_shapes=[pltpu.VMEM((len(DIL_BRANCHES), 2, t, LANES), F32),
                        pltpu.VMEM((len(DIL_BRANCHES), 2, t, 1), F32)],
        compiler_params=_cparams("parallel", "parallel"),
        name="attn_prompt",
    )(q, k, proj_att)


def _attn_sample_kernel(q_ref, kn_ref, vn_ref, ck_ref, cv_ref, o_ref, *, n_ctx, t_new):
    nh = N_HEADS_ATT
    nq = nh * t_new
    q = q_ref[0]
    qt = jnp.concatenate([q] * nh, axis=0)
    rowh = lax.broadcasted_iota(jnp.int32, (nq, D_ATT), 0) // t_new
    laneh = lax.broadcasted_iota(jnp.int32, (nq, D_ATT), 1) // HEAD_DIM
    own = rowh == laneh
    qbd = jnp.where(own, qt, 0.0)

    pad = LANES - t_new
    kn = jnp.concatenate([kn_ref[0], jnp.zeros((pad, D_ATT), F32)], axis=0)
    vn = jnp.concatenate([vn_ref[0], jnp.zeros((pad, D_ATT), F32)], axis=0)

    s_c = _dot_nt(qbd, ck_ref[0])
    s_n = _dot_nt(qbd, kn)

    def multiplicity(d):
        mult = jnp.zeros(d.shape, F32)
        for window, dil in DIL_BRANCHES:
            mult = mult + jnp.where((d >= 0) & (d <= window) & (d % dil == 0), 1.0, 0.0)
        return mult

    qi_c = lax.broadcasted_iota(jnp.int32, (nq, n_ctx), 0) % t_new
    d_c = n_ctx + qi_c - lax.broadcasted_iota(jnp.int32, (nq, n_ctx), 1)
    qi_n = lax.broadcasted_iota(jnp.int32, (nq, LANES), 0) % t_new
    col_n = lax.broadcasted_iota(jnp.int32, (nq, LANES), 1)
    d_n = jnp.where(col_n < t_new, qi_n - col_n, -1)
    mult_c = multiplicity(d_c)
    mult_n = multiplicity(d_n)

    s_c = jnp.where(mult_c > 0, s_c, -jnp.inf)
    s_n = jnp.where(mult_n > 0, s_n, -jnp.inf)
    mx = jnp.maximum(jnp.max(s_c, axis=-1, keepdims=True), jnp.max(s_n, axis=-1, keepdims=True))
    p_c = mult_c * jnp.exp(s_c - mx)
    p_n = mult_n * jnp.exp(s_n - mx)
    den = jnp.sum(p_c, axis=-1, keepdims=True) + jnp.sum(p_n, axis=-1, keepdims=True)
    o_full = (_dot(p_c, cv_ref[0]) + _dot(p_n, vn)) / den
    o_full = jnp.where(own, o_full, 0.0)
    o = o_full[0:t_new]
    for h in range(1, nh):
        o = o + o_full[h * t_new:(h + 1) * t_new]
    o_ref[0] = o.astype(o_ref.dtype)


def _attn_sample(q, k_new, v_new, cache_k, cache_v):
    b, t_new, _ = q.shape
    n_ctx = cache_k.shape[1]
    new = pl.BlockSpec((1, t_new, D_ATT), lambda i: (i, 0, 0))
    ctx = pl.BlockSpec((1, n_ctx, D_ATT), lambda i: (i, 0, 0))
    return pl.pallas_call(
        functools.partial(_attn_sample_kernel, n_ctx=n_ctx, t_new=t_new),
        grid=(b,),
        in_specs=[new, new, new, ctx, ctx],
        out_specs=new,
        out_shape=jax.ShapeDtypeStruct((b, t_new, D_ATT), BF16),
        compiler_params=_cparams("parallel"),
        name="attn_sample",
    )(q, k_new, v_new, cache_k, cache_v)


def _rwkv_prep_kernel(p_ref, sh_ref, mu_ref, w0_ref, dup_ref, a0_ref, iup_ref, gup_ref, kk_ref, ka_ref,
                      r_o, k_o, v_o, kkn_o, beta_o, logd_o, g_o, carry_ref):
    c1, c2, c3 = D_RWKV, 2 * D_RWKV, 3 * D_RWKV
    c4, c5 = c3 + LORA_PAD, c3 + 2 * LORA_PAD

    @pl.when(pl.program_id(1) == 0)
    def _():
        carry_ref[...] = sh_ref[0]

    cur = p_ref[0]
    bt = cur.shape[0]
    prev = pltpu.roll(cur, 1, axis=0)
    row = lax.broadcasted_iota(jnp.int32, (bt, 1), 0)
    prev = jnp.where(row == 0, carry_ref[...], prev)
    carry_ref[...] = cur[bt - 1:bt]
    xm = cur + (prev - cur) * mu_ref[...]

    r, k, v = xm[:, :c1], xm[:, c1:c2], xm[:, c2:c3]
    wd, ad, gd = xm[:, c3:c4], xm[:, c4:c5], xm[:, c5:]

    z = -(w0_ref[...] + jnp.dot(jnp.tanh(wd), dup_ref[...], precision=HIGHEST, preferred_element_type=F32))
    softplus = jnp.maximum(z, 0.0) + jnp.log(1.0 + jnp.exp(-jnp.abs(z)))
    w = -softplus - 0.5
    logd_o[0] = -jnp.exp(w)
    a = jax.nn.sigmoid(a0_ref[...] + jnp.dot(ad, iup_ref[...], precision=HIGHEST,
                                             preferred_element_type=F32))
    g_o[0] = jnp.dot(jax.nn.sigmoid(gd), gup_ref[...], precision=HIGHEST, preferred_element_type=F32)

    kk = k * kk_ref[...]
    e = _head_sum_matrix()
    for c in range(N_PAIRS):
        sl = slice(c * LANES, (c + 1) * LANES)
        kc = kk[:, sl]
        ss = _dot_exact_rhs(kc * kc, e)
        kn = kc * lax.rsqrt(jnp.maximum(ss, 1e-24))
        kkn_o[0, :, sl] = kn
        beta_o[0, :, sl] = kn * a[:, sl]
    r_o[0] = r
    k_o[0] = k * (1.0 + (a - 1.0) * ka_ref[...])
    v_o[0] = v


def _rwkv_prep(proj_rw, shift_proj, prm, b, t, bt):
    blk = pl.BlockSpec((1, bt, D_RW_PROJ), lambda i, j: (i, j, 0))
    full = lambda a: pl.BlockSpec(a.shape, lambda i, j: (0,) * a.ndim)
    out = pl.BlockSpec((1, bt, D_RWKV), lambda i, j: (i, j, 0))
    args = (prm["mu"], prm["w0"], prm["decay_up"], prm["a0"], prm["iclr_up"], prm["gate_up"],
            prm["k_k"], prm["k_a"])
    return pl.pallas_call(
        _rwkv_prep_kernel,
        grid=(b, t // bt),
        in_specs=[blk, pl.BlockSpec((1, 1, D_RW_PROJ), lambda i, j: (i, 0, 0))] + [full(a) for a in args],
        out_specs=[out] * 7,
        out_shape=[jax.ShapeDtypeStruct((b, t, D_RWKV), F32)] * 7,
        scratch_shapes=[pltpu.VMEM((1, D_RW_PROJ), F32)],
        compiler_params=_cparams("parallel", "arbitrary"),
        name="rwkv_prep",
    )(proj_rw, shift_proj, *args)


def _rwkv_chunk_kernel(r_ref, k_ref, v_ref, kkn_ref, beta_ref, logd_ref, g_ref, s0_ref,
                       rk_ref, gnw_ref, gnb_ref, o_ref, sout_ref, s_ref):
    c = CHUNK
    ci = pl.program_id(1)

    @pl.when(ci == 0)
    def _():
        s_ref[...] = s0_ref[0]

    tri = (lax.broadcasted_iota(jnp.int32, (c, c), 0) >= lax.broadcasted_iota(jnp.int32, (c, c), 1)).astype(BF16)
    logd = logd_ref[0]
    l1, l2, l3 = _split3(logd)
    cum = (jnp.dot(tri, l1, preferred_element_type=F32) + jnp.dot(tri, l2, preferred_element_type=F32)
           + jnp.dot(tri, l3, preferred_element_type=F32))
    total = cum[c - 1:c]
    e_in = jnp.exp(cum)
    e_ex = jnp.exp(cum - logd)
    e_neg = jnp.exp(-cum)
    e_end = jnp.exp(total - cum)
    p_end = jnp.exp(total)

    r = r_ref[0]
    k = k_ref[0]
    v = v_ref[0]
    beta = beta_ref[0]
    alpha_t = -kkn_ref[0] * e_ex
    r_t = r * e_in
    beta_t = beta * e_neg
    k_t = k * e_neg
    beta_e = beta * e_end
    k_e = k * e_end

    lane = lax.broadcasted_iota(jnp.int32, (1, LANES), 1)
    lo = lane < HEAD_DIM
    rr = lax.broadcasted_iota(jnp.int32, (2 * c, 2 * c), 0)
    cc = lax.broadcasted_iota(jnp.int32, (2 * c, 2 * c), 1)
    same_head = (rr // c) == (cc // c)
    strict = same_head & (cc < rr)
    incl = same_head & (cc <= rr)

    def expand(x):
        return jnp.concatenate([jnp.where(lo, x, 0.0), jnp.where(lo, 0.0, x)], axis=0)

    def twice(x):
        return jnp.concatenate([x, x], axis=0)

    pairs = range(N_PAIRS)
    sls = [slice(p * LANES, (p + 1) * LANES) for p in pairs]
    v_x = [expand(v[:, sl]).astype(BF16) for sl in sls]
    lx = [jnp.concatenate([expand(alpha_t[:, sl]), expand(r_t[:, sl])], axis=0).astype(BF16) for sl in sls]
    rx = [jnp.concatenate([twice(beta_t[:, sl]), twice(k_t[:, sl])], axis=0).astype(BF16) for sl in sls]
    gmat = [_dot_nt(lx[p], rx[p]) for p in pairs]
    ls = [_dot_nt(lx[p], s_ref[p]) for p in pairs]
    npow = [jnp.where(strict, gmat[p][:2 * c, :2 * c], 0.0).astype(BF16) for p in pairs]
    a_ak = [jnp.where(strict, gmat[p][:2 * c, 2 * c:], 0.0) for p in pairs]
    a_r = [jnp.concatenate([jnp.where(incl, gmat[p][2 * c:, :2 * c], 0.0),
                            jnp.where(incl, gmat[p][2 * c:, 2 * c:], 0.0)], axis=1).astype(BF16) for p in pairs]
    u = [ls[p][:2 * c] + _dot(a_ak[p], v_x[p]) for p in pairs]
    steps = c.bit_length() - 1
    for it in range(steps):
        u = [u[p] + _dot(npow[p], u[p]) for p in pairs]
        if it + 1 < steps:
            npow = [_dot(npow[p], npow[p]).astype(BF16) for p in pairs]
    uv = [jnp.concatenate([u[p].astype(BF16), v_x[p]], axis=0) for p in pairs]
    o_x = [ls[p][2 * c:] + _dot(a_r[p], uv[p]) for p in pairs]
    outs = [o_x[p][:c] + o_x[p][c:] for p in pairs]
    bk = [jnp.concatenate([expand(beta_e[:, sl]), expand(k_e[:, sl])], axis=0).astype(BF16) for sl in sls]
    for p in pairs:
        s_ref[p] = s_ref[p] * p_end[:, sls[p]] + _dot_tn(uv[p], bk[p])

    e = _head_sum_matrix()
    o_all = jnp.concatenate(outs, axis=0)

    def stack(x):
        return jnp.concatenate([x[:, p * LANES:(p + 1) * LANES] for p in range(N_PAIRS)], axis=0)

    def stack_param(ref):
        x = ref[...]
        return jnp.concatenate([jnp.broadcast_to(x[:, p * LANES:(p + 1) * LANES], (c, LANES))
                                for p in range(N_PAIRS)], axis=0)

    mean = _dot_exact_rhs(o_all, e) * (1.0 / HEAD_DIM)
    d = o_all - mean
    var = _dot_exact_rhs(d * d, e) * (1.0 / HEAD_DIM)
    y = d * lax.rsqrt(var + GN_EPS) * stack_param(gnw_ref) + stack_param(gnb_ref)
    bonus = _dot_exact_rhs(stack(r * k) * stack_param(rk_ref), e)
    y = (y + bonus * stack(v)) * stack(g_ref[0])
    for p in range(N_PAIRS):
        o_ref[0, :, p * LANES:(p + 1) * LANES] = y[p * c:(p + 1) * c].astype(o_ref.dtype)

    @pl.when(ci == pl.num_programs(1) - 1)
    def _():
        sout_ref[0] = s_ref[...]


def _rwkv_chunk(r, k, v, kkn, beta, logd, g, s0, prm, b, t):
    blk = pl.BlockSpec((1, CHUNK, D_RWKV), lambda i, j: (i, j, 0))
    st = pl.BlockSpec((1, N_PAIRS, LANES, LANES), lambda i, j: (i, 0, 0, 0))
    par = pl.BlockSpec((1, D_RWKV), lambda i, j: (0, 0))
    return pl.pallas_call(
        _rwkv_chunk_kernel,
        grid=(b, t // CHUNK),
        in_specs=[blk] * 7 + [st, par, par, par],
        out_specs=[blk, st],
        out_shape=[jax.ShapeDtypeStruct((b, t, D_RWKV), BF16),
                   jax.ShapeDtypeStruct((b, N_PAIRS, LANES, LANES), F32)],
        scratch_shapes=[pltpu.VMEM((N_PAIRS, LANES, LANES), F32)],
        compiler_params=_cparams("parallel", "arbitrary"),
        name="rwkv_chunk",
    )(r, k, v, kkn, beta, logd, g, s0, prm["r_k"], prm["gn_w"], prm["gn_b"])


def _state_to_pairs(wkv):
    b = wkv.shape[0]
    w = wkv.reshape(b, N_PAIRS, 2, HEAD_DIM, HEAD_DIM)
    z = jnp.zeros((b, N_PAIRS, HEAD_DIM, HEAD_DIM), wkv.dtype)
    top = jnp.concatenate([w[:, :, 0], z], axis=-1)
    bot = jnp.concatenate([z, w[:, :, 1]], axis=-1)
    return jnp.concatenate([top, bot], axis=-2)


def _pairs_to_state(s):
    b = s.shape[0]
    h0 = s[:, :, :HEAD_DIM, :HEAD_DIM]
    h1 = s[:, :, HEAD_DIM:, HEAD_DIM:]
    return jnp.stack([h0, h1], axis=2).reshape(b, N_HEADS_RWKV, HEAD_DIM, HEAD_DIM)


def _out_router_kernel(x_ref, a_ref, r_ref, wa_ref, wr_ref, g2_ref, rt1_ref, rt2_ref,
                       x2_ref, h2_ref, lg_ref):
    x2 = (x_ref[...] + jnp.dot(a_ref[...], wa_ref[...], preferred_element_type=F32)
          + jnp.dot(r_ref[...], wr_ref[...], preferred_element_type=F32))
    x2_ref[...] = x2
    ms = jnp.mean(x2 * x2, axis=-1, keepdims=True)
    h2 = x2 * lax.rsqrt(ms + RMS_EPS) * g2_ref[...]
    hi, lo = _split2(h2)
    h2_ref[...] = hi
    t12 = jnp.dot(hi, rt1_ref[...], preferred_element_type=F32)
    t3 = jnp.dot(lo, rt2_ref[...], preferred_element_type=F32)
    lg_ref[...] = t12[:, :ROUTER_PAD] + t12[:, ROUTER_PAD:] + t3


def _out_router(x, o_att, o_rwkv, wa, wr, g2, rt1, rt2, bm):
    m, d = x.shape
    row = lambda w: pl.BlockSpec((bm, w), lambda i: (i, 0))
    full = lambda a: pl.BlockSpec(a.shape, lambda i: (0,) * a.ndim)
    return pl.pallas_call(
        _out_router_kernel,
        grid=(m // bm,),
        in_specs=[row(d), row(D_ATT), row(D_RWKV), full(wa), full(wr), full(g2), full(rt1), full(rt2)],
        out_specs=[row(d), row(d), row(ROUTER_PAD)],
        out_shape=[jax.ShapeDtypeStruct((m, d), F32), jax.ShapeDtypeStruct((m, d), BF16),
                   jax.ShapeDtypeStruct((m, ROUTER_PAD), F32)],
        compiler_params=_cparams("parallel"),
        name="out_router",
    )(x, o_att, o_rwkv, wa, wr, g2, rt1, rt2)


def _moe_kernel(te_ref, tx_ref, first_ref, nxt_ref, nv_ref, x_ref, wg_hbm, wu_hbm, wd_hbm, y_ref,
                sg_ref, su_ref, sd_ref, wg_ref, wu_ref, wd_ref, sem):
    i = pl.program_id(0)

    def weight_copies(e):
        return (pltpu.make_async_copy(wg_hbm.at[e], sg_ref, sem.at[0]),
                pltpu.make_async_copy(wu_hbm.at[e], su_ref, sem.at[1]),
                pltpu.make_async_copy(wd_hbm.at[e], sd_ref, sem.at[2]))

    @pl.when(i == 0)
    def _():
        for cp in weight_copies(te_ref[0]):
            cp.start()

    valid = i < nv_ref[0]

    @pl.when(valid & (first_ref[i] == 1))
    def _():
        cps = weight_copies(te_ref[i])
        for cp, stage, dst in zip(cps, (sg_ref, su_ref, sd_ref), (wg_ref, wu_ref, wd_ref)):
            cp.wait()
            dst[...] = stage[...].astype(BF16)

        @pl.when(nxt_ref[i] >= 0)
        def _():
            for cp in weight_copies(nxt_ref[i]):
                cp.start()

    @pl.when(valid)
    def _():
        x = x_ref[...]
        gt = jnp.dot(x, wg_ref[...], preferred_element_type=F32)
        up = jnp.dot(x, wu_ref[...], preferred_element_type=F32)
        hmid = (gt * jax.nn.sigmoid(gt) * up).astype(BF16)
        y_ref[...] = jnp.dot(hmid, wd_ref[...], preferred_element_type=F32)

    @pl.when(jnp.logical_not(valid))
    def _():
        y_ref[...] = jnp.zeros_like(y_ref)


def _moe_ffn(xs, tile_expert, tile_index, tile_first, tile_next, n_valid, wg, wu, wd):
    rows, d = xs.shape
    n_tiles = rows // MOE_TILE
    hbm = pl.BlockSpec(memory_space=pl.ANY)
    grid_spec = pltpu.PrefetchScalarGridSpec(
        num_scalar_prefetch=5,
        grid=(n_tiles,),
        in_specs=[pl.BlockSpec((MOE_TILE, d), lambda i, te, tx, fr, nx, nv: (tx[i], 0)), hbm, hbm, hbm],
        out_specs=pl.BlockSpec((MOE_TILE, d), lambda i, te, tx, fr, nx, nv: (i, 0)),
        scratch_shapes=[pltpu.VMEM((d, D_EXPERT), F32), pltpu.VMEM((d, D_EXPERT), F32),
                        pltpu.VMEM((D_EXPERT, d), F32),
                        pltpu.VMEM((d, D_EXPERT), BF16), pltpu.VMEM((d, D_EXPERT), BF16),
                        pltpu.VMEM((D_EXPERT, d), BF16),
                        pltpu.SemaphoreType.DMA((3,))],
    )
    return pl.pallas_call(
        _moe_kernel,
        grid_spec=grid_spec,
        out_shape=jax.ShapeDtypeStruct((rows, d), F32),
        compiler_params=_cparams("arbitrary"),
        name="moe_ffn",
    )(tile_expert, tile_index, tile_first, tile_next, n_valid, xs, wg, wu, wd)


def _route(logits):
    lg = logits[:, :N_GROUPS]
    pg = jax.nn.softmax(lg, axis=-1)
    grp = jnp.argmax(lg, axis=-1)
    p_grp = jnp.take_along_axis(pg, grp[:, None], axis=-1)
    le = logits[:, N_GROUPS:N_GROUPS + N_EXPERTS].reshape(-1, N_GROUPS, EXPERTS_PER_GROUP)
    le = jnp.take_along_axis(le, grp[:, None, None], axis=1)[:, 0]
    top_p, top_i = lax.top_k(jax.nn.softmax(le, axis=-1), TOP_K_FINE)
    gate = p_grp * top_p / jnp.sum(top_p, axis=-1, keepdims=True)
    eid = grp[:, None] * EXPERTS_PER_GROUP + top_i
    return eid.astype(jnp.int32), gate


RANK_BLOCK = 256


def _moe(h2, logits, wg, wu, wd):
    n, d = h2.shape
    eid, gate = _route(logits)
    m = n * TOP_K_FINE
    eid_f = eid.reshape(-1)
    tok = jnp.repeat(jnp.arange(n, dtype=jnp.int32), TOP_K_FINE)
    experts = jnp.arange(N_EXPERTS, dtype=jnp.int32)
    onehot = eid_f[:, None] == experts[None, :]
    nb = m // RANK_BLOCK
    tri = (jnp.arange(RANK_BLOCK)[:, None] >= jnp.arange(RANK_BLOCK)[None, :]).astype(BF16)
    within = jnp.einsum('ij,bjk->bik', tri, onehot.astype(BF16).reshape(nb, RANK_BLOCK, N_EXPERTS),
                        preferred_element_type=F32)
    block_tot = within[:, -1, :]
    block_off = jnp.cumsum(block_tot, axis=0) - block_tot
    incl = (within + block_off[:, None, :]).reshape(m, N_EXPERTS)
    counts = (block_off[-1] + block_tot[-1]).astype(jnp.int32)
    padded = (counts + MOE_TILE - 1) // MOE_TILE * MOE_TILE
    pend = jnp.cumsum(padded)
    pstart = pend - padded
    start = jnp.cumsum(counts) - counts
    dest = jnp.sum(jnp.where(onehot, incl - 1.0 + pstart.astype(F32)[None, :], 0.0), axis=-1).astype(jnp.int32)
    pos = dest.reshape(n, TOP_K_FINE)

    n_tiles = -(-(m + N_EXPERTS * (MOE_TILE - 1)) // MOE_TILE)
    cap = n_tiles * MOE_TILE
    _, tok_s = lax.sort((dest, tok), num_keys=1)
    row = jnp.arange(cap, dtype=jnp.int32)
    row_e = jnp.minimum(jnp.sum(row[:, None] >= pend[None, :], axis=-1), N_EXPERTS - 1)
    idx = row - pstart[row_e]
    src = jnp.clip(start[row_e] + idx, 0, m - 1)
    buf_tok = jnp.where(idx < counts[row_e], tok_s[src], 0)
    xs = h2[buf_tok]

    n_valid = (pend[-1] // MOE_TILE).astype(jnp.int32)
    tiles = jnp.arange(n_tiles, dtype=jnp.int32)
    tile_index = jnp.minimum(tiles, n_valid - 1)
    tile_expert = jnp.minimum(jnp.sum((tile_index * MOE_TILE)[:, None] >= pend[None, :], axis=-1),
                              N_EXPERTS - 1).astype(jnp.int32)
    prev_expert = jnp.concatenate([jnp.full((1,), -1, jnp.int32), tile_expert[:-1]])
    tile_first = (tile_expert != prev_expert).astype(jnp.int32)
    next_tile = pend[tile_expert] // MOE_TILE
    tile_next = jnp.where(next_tile < n_valid, tile_expert[jnp.minimum(next_tile, n_tiles - 1)], -1)
    ys = _moe_ffn(xs, tile_expert, tile_index, tile_first, tile_next.astype(jnp.int32), n_valid.reshape(1),
                  wg, wu, wd)
    return ys[pos[:, 0]] * gate[:, 0:1] + ys[pos[:, 1]] * gate[:, 1:2]


def _pad_cols(w, width):
    return jnp.pad(w, ((0, 0), (0, width - w.shape[1])))


def _pad_rows(w, height):
    return jnp.pad(w, ((0, height - w.shape[0]), (0, 0)))


def _prepare_params(w_in, tmix_mu, decay_w0, decay_up, iclr_a0, iclr_up, gate_up, k_k, k_a, r_k,
                    gn_w, gn_b, w_out, router_group, router_expert):
    c_att = 3 * D_ATT
    c3 = c_att + 3 * D_RWKV
    c4, c5 = c3 + LORA_DECAY, c3 + LORA_DECAY + LORA_ICLR
    w_rw = jnp.concatenate([w_in[:, c_att:c3], _pad_cols(w_in[:, c3:c4], LORA_PAD),
                            _pad_cols(w_in[:, c4:c5], LORA_PAD), w_in[:, c5:]], axis=1)
    mu = tmix_mu.reshape(1, -1)
    o3 = 3 * D_RWKV
    mu = jnp.concatenate([mu[:, :o3], _pad_cols(mu[:, o3:o3 + LORA_DECAY], LORA_PAD),
                          _pad_cols(mu[:, o3 + LORA_DECAY:o3 + LORA_DECAY + LORA_ICLR], LORA_PAD),
                          mu[:, o3 + LORA_DECAY + LORA_ICLR:]], axis=1)
    router = _pad_cols(jnp.concatenate([router_group, router_expert], axis=1), ROUTER_PAD)
    rt_hi = router.astype(BF16)
    rt_lo = (router - rt_hi.astype(F32)).astype(BF16)
    row = lambda z: z.reshape(1, -1)
    return dict(
        w_att=w_in[:, :c_att].astype(BF16), w_rw=w_rw.astype(BF16), mu=mu,
        w0=row(decay_w0), decay_up=_pad_rows(decay_up, LORA_PAD), a0=row(iclr_a0),
        iclr_up=_pad_rows(iclr_up, LORA_PAD), gate_up=gate_up, k_k=row(k_k), k_a=row(k_a),
        r_k=row(r_k), gn_w=row(gn_w), gn_b=row(gn_b),
        wo_att=w_out[:D_ATT].astype(BF16), wo_rw=w_out[D_ATT:].astype(BF16),
        rt1=jnp.concatenate([rt_hi, rt_lo], axis=1), rt2=rt_hi)


def _mixers(x, pos, shift_row, wkv0, k_ctx, v_ctx, prm, norm1_g, q_norm_g, k_norm_g, bm, bt):
    b, t, d = x.shape
    rows = b * t
    x2d = x.reshape(rows, d)
    g1 = norm1_g.reshape(1, d)
    proj_att = _norm_proj(x2d, g1, prm["w_att"], normalize=True, bm=bm, bn=prm["w_att"].shape[1] // 2,
                          name="proj_att").reshape(b, t, 3 * D_ATT)
    proj_rw = _norm_proj(x2d, g1, prm["w_rw"], normalize=True, bm=bm, bn=D_RW_PROJ // 2,
                         name="proj_rw").reshape(b, t, D_RW_PROJ)
    if shift_row is None:
        shift_proj = jnp.zeros((b, 1, D_RW_PROJ), F32)
    else:
        shift_proj = _norm_proj(shift_row, g1, prm["w_rw"], normalize=False, bm=b, bn=D_RW_PROJ // 2,
                                name="proj_shift").reshape(b, 1, D_RW_PROJ)

    q, k = _qk_prep(proj_att, pos, q_norm_g, k_norm_g, b, t, bt)
    v = proj_att[:, :, 2 * D_ATT:]
    if k_ctx is None:
        o_att = _attn_prompt(q, k, proj_att, b, t)
    else:
        o_att = _attn_sample(q, k, v, k_ctx, v_ctx)

    r, kr, vr, kkn, beta, logd, g = _rwkv_prep(proj_rw, shift_proj, prm, b, t, bt)
    s0 = _state_to_pairs(wkv0)
    tp = -(-t // CHUNK) * CHUNK
    if tp != t:
        padt = lambda z: jnp.pad(z, ((0, 0), (0, tp - t), (0, 0)))
        r, kr, vr, kkn, beta, logd, g = (padt(z) for z in (r, kr, vr, kkn, beta, logd, g))
    o_rwkv, s_new = _rwkv_chunk(r, kr, vr, kkn, beta, logd, g, s0, prm, b, tp)
    o_rwkv = o_rwkv[:, :t]
    return (o_att.reshape(rows, D_ATT), o_rwkv.reshape(rows, D_RWKV), k, v, _pairs_to_state(s_new))


def kernel(x_prompt, x_sample, cache_k_win, cache_v_win, state_wkv, state_shift, norm1_g, w_in, q_norm_g,
           k_norm_g, tmix_mu, decay_w0, decay_up, iclr_a0, iclr_up, gate_up, k_k, k_a, r_k, gn_w, gn_b,
           w_out, norm2_g, router_group, router_expert, moe_w_gate, moe_w_up, moe_w_down):
    depth = w_in.shape[0]
    assert depth == 1
    l = 0
    b_p, s_p, d = x_prompt.shape
    b_s, t_s, _ = x_sample.shape
    n_ctx = cache_k_win.shape[2]
    prm = _prepare_params(w_in[l], tmix_mu[l], decay_w0[l], decay_up[l], iclr_a0[l], iclr_up[l],
                          gate_up[l], k_k[l], k_a[l], r_k[l], gn_w[l], gn_b[l], w_out[l],
                          router_group[l], router_expert[l])
    g2 = norm2_g[l].reshape(1, d)

    pos_p = jnp.arange(s_p, dtype=F32)
    pos_s = PAST_LEN + jnp.arange(t_s, dtype=F32)
    wkv_zero = jnp.zeros((b_p, N_HEADS_RWKV, HEAD_DIM, HEAD_DIM), F32)

    oa_p, or_p, k_p, v_p, wkv_p = _mixers(x_prompt, pos_p, None, wkv_zero, None, None, prm,
                                          norm1_g[l], q_norm_g[l], k_norm_g[l], bm=512, bt=256)
    ck = cache_k_win[l].reshape(b_s, n_ctx, D_ATT)
    cv = cache_v_win[l].reshape(b_s, n_ctx, D_ATT)
    oa_s, or_s, k_s, v_s, wkv_s = _mixers(x_sample, pos_s, state_shift[l], state_wkv[l], ck, cv, prm,
                                          norm1_g[l], q_norm_g[l], k_norm_g[l], bm=b_s * t_s, bt=t_s)

    n_p, n_s = b_p * s_p, b_s * t_s
    x2_p, h2_p, lg_p = _out_router(x_prompt.reshape(n_p, d), oa_p, or_p, prm["wo_att"], prm["wo_rw"], g2,
                                   prm["rt1"], prm["rt2"], bm=512)
    x2_s, h2_s, lg_s = _out_router(x_sample.reshape(n_s, d), oa_s, or_s, prm["wo_att"], prm["wo_rw"], g2,
                                   prm["rt1"], prm["rt2"], bm=n_s)

    h2 = jnp.concatenate([h2_p, h2_s], axis=0)
    lg = jnp.concatenate([lg_p, lg_s], axis=0)
    moe = _moe(h2, lg, moe_w_gate[l], moe_w_up[l], moe_w_down[l])
    y_p = (x2_p + moe[:n_p]).reshape(b_p, s_p, d)
    y_s = (x2_s + moe[n_p:]).reshape(b_s, t_s, d)

    last = jnp.concatenate([x_prompt[:, -1], x_sample[:, -1]], axis=0)
    shift = _rms_rows(last, norm1_g[l].reshape(1, d))

    keep = min(MAX_WINDOW, s_p)
    heads = lambda z, b, t: z.reshape(1, b, t, N_HEADS_ATT, HEAD_DIM)
    return (y_p, y_s,
            heads(k_p[:, s_p - keep:], b_p, keep), heads(v_p[:, s_p - keep:], b_p, keep),
            wkv_p[None], shift[:b_p][None],
            heads(k_s, b_s, t_s), heads(v_s, b_s, t_s), wkv_s[None], shift[b_p:][None])
```

```python
import functools

import jax
import jax.numpy as jnp
from jax import lax
from jax.experimental import pallas as pl
from jax.experimental.pallas import tpu as pltpu

D_MODEL = 2048
HEAD_DIM = 64
N_HEADS_ATT = 12
N_HEADS_RWKV = 20
D_ATT = N_HEADS_ATT * HEAD_DIM
D_RWKV = N_HEADS_RWKV * HEAD_DIM
DIL_BRANCHES = ((128, 1), (512, 4), (2048, 16))
MAX_WINDOW = 2048
BAND_BLOCK = 128
ROT_DIM = HEAD_DIM // 4
ROPE_THETA = 500000.0
LORA_DECAY = 96
LORA_ICLR = 96
LORA_GATE = 256
GN_EPS = 64e-5
RMS_EPS = 1e-6
N_GROUPS = 8
EXPERTS_PER_GROUP = 8
N_EXPERTS = N_GROUPS * EXPERTS_PER_GROUP
TOP_K_FINE = 2
D_EXPERT = 1024
PAST_LEN = 16384

LANES = 128
SUBLANES = 8
VMEM_LIMIT_BYTES = 56 * 1024 * 1024

LORA_PAD = LANES
D_RW_PROJ = 3 * D_RWKV + 2 * LORA_PAD + LORA_GATE
N_PAIRS = D_RWKV // LANES
CHUNK = 64
ROUTER_PAD = LANES
MOE_TILE = 256
ATTN_GROUP = 4

F32 = jnp.float32
BF16 = jnp.bfloat16
HIGHEST = lax.Precision.HIGHEST


def _cparams(*sem):
    return pltpu.CompilerParams(dimension_semantics=("arbitrary",) * len(sem),
                                vmem_limit_bytes=VMEM_LIMIT_BYTES)


def _dot(a, b):
    return jnp.dot(a.astype(BF16), b.astype(BF16), preferred_element_type=F32)


def _dot_nt(a, b):
    return lax.dot_general(a.astype(BF16), b.astype(BF16), (((1,), (1,)), ((), ())),
                           preferred_element_type=F32)


def _dot_tn(a, b):
    return lax.dot_general(a.astype(BF16), b.astype(BF16), (((0,), (0,)), ((), ())),
                           preferred_element_type=F32)


def _split2(x):
    hi = x.astype(BF16)
    lo = (x - hi.astype(F32)).astype(BF16)
    return hi, lo


def _split3(x):
    hi = x.astype(BF16)
    r = x - hi.astype(F32)
    mid = r.astype(BF16)
    lo = (r - mid.astype(F32)).astype(BF16)
    return hi, mid, lo


def _dot3(a, b):
    a_hi, a_lo = _split2(a)
    b_hi, b_lo = _split2(b)
    return (jnp.dot(a_hi, b_hi, preferred_element_type=F32) + jnp.dot(a_hi, b_lo, preferred_element_type=F32)
            + jnp.dot(a_lo, b_hi, preferred_element_type=F32))


def _dot_exact_rhs(x, e_bf16):
    hi, lo = _split2(x)
    return (jnp.dot(hi, e_bf16, preferred_element_type=F32)
            + jnp.dot(lo, e_bf16, preferred_element_type=F32))


def _head_sum_matrix(width=LANES):
    r = lax.broadcasted_iota(jnp.int32, (width, width), 0) // HEAD_DIM
    c = lax.broadcasted_iota(jnp.int32, (width, width), 1) // HEAD_DIM
    return (r == c).astype(BF16)


def _proj_kernel(x_ref, g_ref, w_ref, o_ref, h_ref, *, normalize):
    @pl.when(pl.program_id(1) == 0)
    def _():
        x = x_ref[...]
        if normalize:
            ms = jnp.mean(x * x, axis=-1, keepdims=True)
            x = x * lax.rsqrt(ms + RMS_EPS) * g_ref[...]
        h_ref[...] = x.astype(BF16)

    o_ref[...] = jnp.dot(h_ref[...], w_ref[...], preferred_element_type=F32)


def _norm_proj(x, g, w_bf16, *, normalize, bm, bn, name):
    m, d = x.shape
    n = w_bf16.shape[1]
    return pl.pallas_call(
        functools.partial(_proj_kernel, normalize=normalize),
        grid=(m // bm, n // bn),
        in_specs=[pl.BlockSpec((bm, d), lambda i, j: (i, 0)),
                  pl.BlockSpec((1, d), lambda i, j: (0, 0)),
                  pl.BlockSpec((d, bn), lambda i, j: (0, j))],
        out_specs=pl.BlockSpec((bm, bn), lambda i, j: (i, j)),
        out_shape=jax.ShapeDtypeStruct((m, n), F32),
        scratch_shapes=[pltpu.VMEM((bm, d), BF16)],
        compiler_params=_cparams("parallel", "arbitrary"),
        name=name,
    )(x, g, w_bf16)


def _rms_rows_kernel(x_ref, g_ref, o_ref):
    x = x_ref[...]
    ms = jnp.mean(x * x, axis=-1, keepdims=True)
    o_ref[...] = x * lax.rsqrt(ms + RMS_EPS) * g_ref[...]


def _rms_rows(x, g):
    return pl.pallas_call(
        _rms_rows_kernel,
        out_shape=jax.ShapeDtypeStruct(x.shape, F32),
        name="rms_rows",
    )(x, g)


def _qk_prep_kernel(q_ref, k_ref, cos_ref, sa_ref, sb_ref, qg_ref, kg_ref, qo_ref, ko_ref):
    e = _head_sum_matrix()
    cos = cos_ref[...]
    sa = sa_ref[...]
    sb = sb_ref[...]

    def norm_rope(x, g):
        ss = _dot_exact_rhs(x * x, e)
        y = x * lax.rsqrt(ss * (1.0 / HEAD_DIM) + RMS_EPS) * g
        nxt = pltpu.roll(y, LANES - ROT_DIM // 2, axis=1)
        prv = pltpu.roll(y, ROT_DIM // 2, axis=1)
        return y * cos + nxt * sa + prv * sb

    for c in range(D_ATT // LANES):
        sl = slice(c * LANES, (c + 1) * LANES)
        qo_ref[0, :, sl] = norm_rope(q_ref[0, :, sl], qg_ref[...]) * (HEAD_DIM ** -0.5)
        ko_ref[0, :, sl] = norm_rope(k_ref[0, :, sl], kg_ref[...])


def _rope_tables(pos):
    half = ROT_DIM // 2
    inv_freq = ROPE_THETA ** (-2.0 * jnp.arange(half, dtype=F32) / ROT_DIM)
    ang = pos[:, None] * inv_freq[None, :]
    cos, sin = jnp.cos(ang), jnp.sin(ang)
    t = pos.shape[0]
    one = jnp.ones((t, HEAD_DIM - ROT_DIM), F32)
    zero = jnp.zeros((t, HEAD_DIM - ROT_DIM), F32)
    zh = jnp.zeros((t, half), F32)
    cos_h = jnp.concatenate([cos, cos, one], axis=1)
    sa_h = jnp.concatenate([-sin, zh, zero], axis=1)
    sb_h = jnp.concatenate([zh, sin, zero], axis=1)
    rep = LANES // HEAD_DIM
    return tuple(jnp.tile(z, (1, rep)) for z in (cos_h, sa_h, sb_h))


def _qk_prep(proj_att, pos, q_norm_g, k_norm_g, b, t, bt):
    cos, sa, sb = _rope_tables(pos)
    rep = LANES // HEAD_DIM
    qg = jnp.tile(q_norm_g.reshape(1, HEAD_DIM), (1, rep))
    kg = jnp.tile(k_norm_g.reshape(1, HEAD_DIM), (1, rep))
    tab = pl.BlockSpec((bt, LANES), lambda i, j: (j, 0))
    gsp = pl.BlockSpec((1, LANES), lambda i, j: (0, 0))
    return pl.pallas_call(
        _qk_prep_kernel,
        grid=(b, t // bt),
        in_specs=[pl.BlockSpec((1, bt, D_ATT), lambda i, j: (i, j, 0)),
                  pl.BlockSpec((1, bt, D_ATT), lambda i, j: (i, j, 1)),
                  tab, tab, tab, gsp, gsp],
        out_specs=[pl.BlockSpec((1, bt, D_ATT), lambda i, j: (i, j, 0)),
                   pl.BlockSpec((1, bt, D_ATT), lambda i, j: (i, j, 0))],
        out_shape=[jax.ShapeDtypeStruct((b, t, D_ATT), F32)] * 2,
        compiler_params=_cparams("parallel", "arbitrary"),
        name="qk_prep",
    )(proj_att, proj_att, cos, sa, sb, qg, kg)


def _attn_prompt_kernel(q_ref, k_ref, v_ref, o_ref, acc_ref, m_ref, *, seq):
    blk = BAND_BLOCK
    lane = lax.broadcasted_iota(jnp.int32, (1, LANES), 1)
    lo = lane < HEAD_DIM
    n_units = seq // blk
    qi = lax.broadcasted_iota(jnp.int32, (2 * blk, 1), 0) % blk

    for g, (window, dil) in enumerate(DIL_BRANCHES):
        n_back = window // dil
        n_blk = seq // dil // blk
        with_prev = n_blk > 1
        nk = 2 * blk if with_prev else blk
        kj = lax.broadcasted_iota(jnp.int32, (1, nk), 1)
        dist = qi + (blk if with_prev else 0) - kj
        in_band = (dist >= 0) & (dist <= n_back)

        def group(gi, carry, g=g, dil=dil, with_prev=with_prev, in_band=in_band, kj=kj):
            units = []
            for j in range(ATTN_GROUP):
                u = gi * ATTN_GROUP + j
                r = u % dil
                n = u // dil
                start = r + n * (blk * dil)
                rows = pl.ds(start, blk, stride=dil)
                q = q_ref[0, rows, :]
                k = k_ref[0, rows, :]
                v = v_ref[0, rows, :]
                valid = in_band
                if with_prev:
                    pstart = jnp.maximum(start - blk * dil, r)
                    prows = pl.ds(pstart, blk, stride=dil)
                    k = jnp.concatenate([k_ref[0, prows, :], k], axis=0)
                    v = jnp.concatenate([v_ref[0, prows, :], v], axis=0)
                    valid = valid & ((kj >= blk) | (n > 0))
                qq = jnp.concatenate([jnp.where(lo, q, 0.0), jnp.where(lo, 0.0, q)], axis=0)
                units.append((rows, valid, qq.astype(BF16), k.astype(BF16), v))
            scores = [_dot_nt(qq, k) for _, _, qq, k, _ in units]
            probs = []
            for (rows, valid, _, _, _), s in zip(units, scores):
                s = jnp.where(valid, s, -jnp.inf)
                m = jnp.max(s, axis=-1, keepdims=True)
                probs.append((m, jnp.exp(s - m).astype(BF16)))
            accs = [(_dot(p[:blk], jnp.where(lo, v, 1.0)), _dot(p[blk:], jnp.where(lo, 1.0, v)))
                    for (_, _, _, _, v), (_, p) in zip(units, probs)]
            for (rows, _, _, _, _), (m, _), (a0, a1) in zip(units, probs, accs):
                acc_ref[g, 0, rows, :] = a0
                acc_ref[g, 1, rows, :] = a1
                m_ref[g, 0, rows, :] = jnp.broadcast_to(m[:blk], (blk, LANES))
                m_ref[g, 1, rows, :] = jnp.broadcast_to(m[blk:], (blk, LANES))
            return carry

        lax.fori_loop(0, n_units // ATTN_GROUP, group, 0)

    mb = 2 * blk

    def merge(i, carry):
        rows = pl.ds(pl.multiple_of(i * mb, mb), mb)
        out = None
        for e in range(LANES // HEAD_DIM):
            ms = [m_ref[g, e, rows, :] for g in range(len(DIL_BRANCHES))]
            mx = functools.reduce(jnp.maximum, ms)
            tot = None
            for g in range(len(DIL_BRANCHES)):
                term = jnp.exp(ms[g] - mx) * acc_ref[g, e, rows, :]
                tot = term if tot is None else tot + term
            den = pltpu.roll(tot, HEAD_DIM, axis=1)
            oe = tot / den
            out = oe if out is None else jnp.where((lane // HEAD_DIM) == e, oe, out)
        o_ref[0, rows, :] = out.astype(o_ref.dtype)
        return carry

    lax.fori_loop(0, seq // mb, merge, 0, unroll=2)


def _attn_prompt(q, k, proj_att, b, t):
    n_pairs = D_ATT // LANES
    v_off = 2 * D_ATT // LANES
    spec = pl.BlockSpec((1, t, LANES), lambda i, j: (i, 0, j))
    return pl.pallas_call(
        functools.partial(_attn_prompt_kernel, seq=t),
        grid=(b, n_pairs),
        in_specs=[spec, spec, pl.BlockSpec((1, t, LANES), lambda i, j: (i, 0, v_off + j))],
        out_specs=spec,
        out_shape=jax.ShapeDtypeStruct((b, t, D_ATT), BF16),
        scratch_shapes=[pltpu.VMEM((len(DIL_BRANCHES), 2, t, LANES), F32),
                        pltpu.VMEM((len(DIL_BRANCHES), 2, t, LANES), F32)],
        compiler_params=_cparams("parallel", "parallel"),
        name="attn_prompt",
    )(q, k, proj_att)


def _attn_sample_kernel(q_ref, kn_ref, vn_ref, ck_ref, cv_ref, o_ref, *, n_ctx, t_new):
    nh = N_HEADS_ATT
    nq = nh * t_new
    q = q_ref[0]
    qt = jnp.concatenate([q] * nh, axis=0)
    rowh = lax.broadcasted_iota(jnp.int32, (nq, D_ATT), 0) // t_new
    laneh = lax.broadcasted_iota(jnp.int32, (nq, D_ATT), 1) // HEAD_DIM
    own = rowh == laneh
    qbd = jnp.where(own, qt, 0.0)

    pad = LANES - t_new
    kn = jnp.concatenate([kn_ref[0], jnp.zeros((pad, D_ATT), F32)], axis=0)
    vn = jnp.concatenate([vn_ref[0], jnp.zeros((pad, D_ATT), F32)], axis=0)

    s_c = _dot_nt(qbd, ck_ref[0])
    s_n = _dot_nt(qbd, kn)

    def multiplicity(d):
        mult = jnp.zeros(d.shape, F32)
        for window, dil in DIL_BRANCHES:
            mult = mult + jnp.where((d >= 0) & (d <= window) & (d % dil == 0), 1.0, 0.0)
        return mult

    qi_c = lax.broadcasted_iota(jnp.int32, (nq, n_ctx), 0) % t_new
    d_c = n_ctx + qi_c - lax.broadcasted_iota(jnp.int32, (nq, n_ctx), 1)
    qi_n = lax.broadcasted_iota(jnp.int32, (nq, LANES), 0) % t_new
    col_n = lax.broadcasted_iota(jnp.int32, (nq, LANES), 1)
    d_n = jnp.where(col_n < t_new, qi_n - col_n, -1)
    mult_c = multiplicity(d_c)
    mult_n = multiplicity(d_n)

    s_c = jnp.where(mult_c > 0, s_c, -jnp.inf)
    s_n = jnp.where(mult_n > 0, s_n, -jnp.inf)
    mx = jnp.maximum(jnp.max(s_c, axis=-1, keepdims=True), jnp.max(s_n, axis=-1, keepdims=True))
    p_c = mult_c * jnp.exp(s_c - mx)
    p_n = mult_n * jnp.exp(s_n - mx)
    den = jnp.sum(p_c, axis=-1, keepdims=True) + jnp.sum(p_n, axis=-1, keepdims=True)
    o_full = (_dot(p_c, cv_ref[0]) + _dot(p_n, vn)) / den
    o_full = jnp.where(own, o_full, 0.0)
    o = o_full[0:t_new]
    for h in range(1, nh):
        o = o + o_full[h * t_new:(h + 1) * t_new]
    o_ref[0] = o.astype(o_ref.dtype)


def _attn_sample(q, k_new, v_new, cache_k, cache_v):
    b, t_new, _ = q.shape
    n_ctx = cache_k.shape[1]
    new = pl.BlockSpec((1, t_new, D_ATT), lambda i: (i, 0, 0))
    ctx = pl.BlockSpec((1, n_ctx, D_ATT), lambda i: (i, 0, 0))
    return pl.pallas_call(
        functools.partial(_attn_sample_kernel, n_ctx=n_ctx, t_new=t_new),
        grid=(b,),
        in_specs=[new, new, new, ctx, ctx],
        out_specs=new,
        out_shape=jax.ShapeDtypeStruct((b, t_new, D_ATT), BF16),
        compiler_params=_cparams("parallel"),
        name="attn_sample",
    )(q, k_new, v_new, cache_k, cache_v)


def _rwkv_prep_kernel(p_ref, sh_ref, mu_ref, w0_ref, dup_ref, a0_ref, iup_ref, gup_ref, kk_ref, ka_ref,
                      r_o, k_o, v_o, kkn_o, beta_o, logd_o, g_o, carry_ref):
    c1, c2, c3 = D_RWKV, 2 * D_RWKV, 3 * D_RWKV
    c4, c5 = c3 + LORA_PAD, c3 + 2 * LORA_PAD

    @pl.when(pl.program_id(1) == 0)
    def _():
        carry_ref[...] = sh_ref[0]

    cur = p_ref[0]
    bt = cur.shape[0]
    prev = pltpu.roll(cur, 1, axis=0)
    row = lax.broadcasted_iota(jnp.int32, (bt, 1), 0)
    prev = jnp.where(row == 0, carry_ref[...], prev)
    carry_ref[...] = cur[bt - 1:bt]
    xm = cur + (prev - cur) * mu_ref[...]

    r, k, v = xm[:, :c1], xm[:, c1:c2], xm[:, c2:c3]
    wd, ad, gd = xm[:, c3:c4], xm[:, c4:c5], xm[:, c5:]

    z = -(w0_ref[...] + _dot3(jnp.tanh(wd), dup_ref[...]))
    softplus = jnp.maximum(z, 0.0) + jnp.log(1.0 + jnp.exp(-jnp.abs(z)))
    w = -softplus - 0.5
    logd_o[0] = -jnp.exp(w)
    a = jax.nn.sigmoid(a0_ref[...] + _dot3(ad, iup_ref[...]))
    g_o[0] = _dot3(jax.nn.sigmoid(gd), gup_ref[...])

    kk = k * kk_ref[...]
    e = _head_sum_matrix()
    for c in range(N_PAIRS):
        sl = slice(c * LANES, (c + 1) * LANES)
        kc = kk[:, sl]
        ss = _dot_exact_rhs(kc * kc, e)
        kn = kc * lax.rsqrt(jnp.maximum(ss, 1e-24))
        kkn_o[0, :, sl] = kn
        beta_o[0, :, sl] = kn * a[:, sl]
    r_o[0] = r
    k_o[0] = k * (1.0 + (a - 1.0) * ka_ref[...])
    v_o[0] = v


def _rwkv_prep(proj_rw, shift_proj, prm, b, t, bt):
    blk = pl.BlockSpec((1, bt, D_RW_PROJ), lambda i, j: (i, j, 0))
    full = lambda a: pl.BlockSpec(a.shape, lambda i, j: (0,) * a.ndim)
    out = pl.BlockSpec((1, bt, D_RWKV), lambda i, j: (i, j, 0))
    args = (prm["mu"], prm["w0"], prm["decay_up"], prm["a0"], prm["iclr_up"], prm["gate_up"],
            prm["k_k"], prm["k_a"])
    return pl.pallas_call(
        _rwkv_prep_kernel,
        grid=(b, t // bt),
        in_specs=[blk, pl.BlockSpec((1, 1, D_RW_PROJ), lambda i, j: (i, 0, 0))] + [full(a) for a in args],
        out_specs=[out] * 7,
        out_shape=[jax.ShapeDtypeStruct((b, t, D_RWKV), F32)] * 7,
        scratch_shapes=[pltpu.VMEM((1, D_RW_PROJ), F32)],
        compiler_params=_cparams("parallel", "arbitrary"),
        name="rwkv_prep",
    )(proj_rw, shift_proj, *args)


def _rwkv_chunk_kernel(r_ref, k_ref, v_ref, kkn_ref, beta_ref, logd_ref, g_ref, s0_ref,
                       rk_ref, gnw_ref, gnb_ref, o_ref, sout_ref, s_ref):
    c = CHUNK
    ci = pl.program_id(1)

    @pl.when(ci == 0)
    def _():
        s_ref[...] = s0_ref[0]

    tri = (lax.broadcasted_iota(jnp.int32, (c, c), 0) >= lax.broadcasted_iota(jnp.int32, (c, c), 1)).astype(BF16)
    logd = logd_ref[0]
    l1, l2, l3 = _split3(logd)
    cum = (jnp.dot(tri, l1, preferred_element_type=F32) + jnp.dot(tri, l2, preferred_element_type=F32)
           + jnp.dot(tri, l3, preferred_element_type=F32))
    total = cum[c - 1:c]
    e_in = jnp.exp(cum)
    e_ex = jnp.exp(cum - logd)
    e_neg = jnp.exp(-cum)
    e_end = jnp.exp(total - cum)
    p_end = jnp.exp(total)

    r = r_ref[0]
    k = k_ref[0]
    v = v_ref[0]
    beta = beta_ref[0]
    alpha_t = -kkn_ref[0] * e_ex
    r_t = r * e_in
    beta_t = beta * e_neg
    k_t = k * e_neg
    beta_e = beta * e_end
    k_e = k * e_end

    lane = lax.broadcasted_iota(jnp.int32, (1, LANES), 1)
    lo = lane < HEAD_DIM
    tt = lax.broadcasted_iota(jnp.int32, (c, 2 * c), 0)
    ss = lax.broadcasted_iota(jnp.int32, (c, 2 * c), 1) % c
    strict = ss < tt
    incl = ss <= tt
    same_head = (lax.broadcasted_iota(jnp.int32, (LANES, LANES), 0) // HEAD_DIM
                 == lax.broadcasted_iota(jnp.int32, (LANES, LANES), 1) // HEAD_DIM)

    def expand(x):
        return jnp.concatenate([jnp.where(lo, x, 0.0), jnp.where(lo, 0.0, x)], axis=0)

    pairs = range(N_PAIRS)
    sls = [slice(p * LANES, (p + 1) * LANES) for p in pairs]
    v_c = [v[:, sl].astype(BF16) for sl in sls]
    v_x = [expand(v[:, sl]).astype(BF16) for sl in sls]
    lc = [jnp.concatenate([alpha_t[:, sl], r_t[:, sl]], axis=0).astype(BF16) for sl in sls]
    rx = [jnp.concatenate([expand(beta_t[:, sl]), expand(k_t[:, sl])], axis=0).astype(BF16) for sl in sls]
    gl = [_dot_nt(lc[p], jnp.concatenate([rx[p], s_ref[p].astype(BF16)], axis=0)) for p in pairs]
    npow = [jnp.where(strict, gl[p][:c, :2 * c], 0.0).astype(BF16) for p in pairs]
    a_ak = [jnp.where(strict, gl[p][:c, 2 * c:4 * c], 0.0) for p in pairs]
    a_r = [jnp.concatenate([jnp.where(incl, gl[p][c:, :2 * c], 0.0),
                            jnp.where(incl, gl[p][c:, 2 * c:4 * c], 0.0)], axis=1).astype(BF16) for p in pairs]
    u = [gl[p][:c, 4 * c:] + _dot(a_ak[p], v_x[p]) for p in pairs]
    steps = c.bit_length() - 1
    for it in range(steps):
        if it + 1 < steps:
            rhs = [jnp.concatenate([expand(u[p]).astype(BF16), expand(npow[p])], axis=1) for p in pairs]
            both = [_dot(npow[p], rhs[p]) for p in pairs]
            u = [u[p] + both[p][:, :2 * c] for p in pairs]
            npow = [both[p][:, 2 * c:].astype(BF16) for p in pairs]
        else:
            u = [u[p] + _dot(npow[p], expand(u[p])) for p in pairs]
    u_b = [x.astype(BF16) for x in u]
    outs = [gl[p][c:, 4 * c:] + _dot(a_r[p], jnp.concatenate([expand(u_b[p]), v_x[p]], axis=0)) for p in pairs]
    for p in pairs:
        sl = sls[p]
        upd = _dot_tn(jnp.concatenate([u_b[p], v_c[p]], axis=0),
                      jnp.concatenate([beta_e[:, sl], k_e[:, sl]], axis=0))
        s_ref[p] = s_ref[p] * p_end[:, sl] + jnp.where(same_head, upd, 0.0)

    gw = 2 * LANES
    n_grp = D_RWKV // gw
    e = _head_sum_matrix(gw)
    o_all = jnp.concatenate([jnp.concatenate([outs[2 * q], outs[2 * q + 1]], axis=1) for q in range(n_grp)],
                            axis=0)

    def stack(x):
        return jnp.concatenate([x[:, q * gw:(q + 1) * gw] for q in range(n_grp)], axis=0)

    def stack_param(ref):
        x = ref[...]
        return jnp.concatenate([jnp.broadcast_to(x[:, q * gw:(q + 1) * gw], (c, gw)) for q in range(n_grp)],
                               axis=0)

    mean = _dot_exact_rhs(o_all, e) * (1.0 / HEAD_DIM)
    d = o_all - mean
    var = _dot_exact_rhs(d * d, e) * (1.0 / HEAD_DIM)
    y = d * lax.rsqrt(var + GN_EPS) * stack_param(gnw_ref) + stack_param(gnb_ref)
    bonus = _dot_exact_rhs(stack(r * k) * stack_param(rk_ref), e)
    y = (y + bonus * stack(v)) * stack(g_ref[0])
    for q in range(n_grp):
        o_ref[0, :, q * gw:(q + 1) * gw] = y[q * c:(q + 1) * c].astype(o_ref.dtype)

    @pl.when(ci == pl.num_programs(1) - 1)
    def _():
        sout_ref[0] = s_ref[...]


def _rwkv_chunk(r, k, v, kkn, beta, logd, g, s0, prm, b, t):
    blk = pl.BlockSpec((1, CHUNK, D_RWKV), lambda i, j: (i, j, 0))
    st = pl.BlockSpec((1, N_PAIRS, LANES, LANES), lambda i, j: (i, 0, 0, 0))
    par = pl.BlockSpec((1, D_RWKV), lambda i, j: (0, 0))
    return pl.pallas_call(
        _rwkv_chunk_kernel,
        grid=(b, t // CHUNK),
        in_specs=[blk] * 7 + [st, par, par, par],
        out_specs=[blk, st],
        out_shape=[jax.ShapeDtypeStruct((b, t, D_RWKV), BF16),
                   jax.ShapeDtypeStruct((b, N_PAIRS, LANES, LANES), F32)],
        scratch_shapes=[pltpu.VMEM((N_PAIRS, LANES, LANES), F32)],
        compiler_params=_cparams("parallel", "arbitrary"),
        name="rwkv_chunk",
    )(r, k, v, kkn, beta, logd, g, s0, prm["r_k"], prm["gn_w"], prm["gn_b"])


def _state_to_pairs(wkv):
    b = wkv.shape[0]
    w = wkv.reshape(b, N_PAIRS, 2, HEAD_DIM, HEAD_DIM)
    z = jnp.zeros((b, N_PAIRS, HEAD_DIM, HEAD_DIM), wkv.dtype)
    top = jnp.concatenate([w[:, :, 0], z], axis=-1)
    bot = jnp.concatenate([z, w[:, :, 1]], axis=-1)
    return jnp.concatenate([top, bot], axis=-2)


def _pairs_to_state(s):
    b = s.shape[0]
    h0 = s[:, :, :HEAD_DIM, :HEAD_DIM]
    h1 = s[:, :, HEAD_DIM:, HEAD_DIM:]
    return jnp.stack([h0, h1], axis=2).reshape(b, N_HEADS_RWKV, HEAD_DIM, HEAD_DIM)


def _out_router_kernel(x_ref, a_ref, r_ref, wa_ref, wr_ref, g2_ref, rt1_ref, rt2_ref,
                       x2_ref, h2_ref, lg_ref):
    x2 = (x_ref[...] + jnp.dot(a_ref[...], wa_ref[...], preferred_element_type=F32)
          + jnp.dot(r_ref[...], wr_ref[...], preferred_element_type=F32))
    x2_ref[...] = x2
    ms = jnp.mean(x2 * x2, axis=-1, keepdims=True)
    h2 = x2 * lax.rsqrt(ms + RMS_EPS) * g2_ref[...]
    hi, lo = _split2(h2)
    h2_ref[...] = hi
    t12 = jnp.dot(hi, rt1_ref[...], preferred_element_type=F32)
    t3 = jnp.dot(lo, rt2_ref[...], preferred_element_type=F32)
    lg_ref[...] = t12[:, :ROUTER_PAD] + t12[:, ROUTER_PAD:] + t3


def _out_router(x, o_att, o_rwkv, wa, wr, g2, rt1, rt2, bm):
    m, d = x.shape
    row = lambda w: pl.BlockSpec((bm, w), lambda i: (i, 0))
    full = lambda a: pl.BlockSpec(a.shape, lambda i: (0,) * a.ndim)
    return pl.pallas_call(
        _out_router_kernel,
        grid=(m // bm,),
        in_specs=[row(d), row(D_ATT), row(D_RWKV), full(wa), full(wr), full(g2), full(rt1), full(rt2)],
        out_specs=[row(d), row(d), row(ROUTER_PAD)],
        out_shape=[jax.ShapeDtypeStruct((m, d), F32), jax.ShapeDtypeStruct((m, d), BF16),
                   jax.ShapeDtypeStruct((m, ROUTER_PAD), F32)],
        compiler_params=_cparams("parallel"),
        name="out_router",
    )(x, o_att, o_rwkv, wa, wr, g2, rt1, rt2)


def _moe_kernel(te_ref, tx_ref, first_ref, nxt_ref, nv_ref, x_ref, wg_hbm, wu_hbm, wd_hbm, y_ref,
                sg_ref, su_ref, sd_ref, wg_ref, wu_ref, wd_ref, sem):
    i = pl.program_id(0)

    def weight_copies(e):
        return (pltpu.make_async_copy(wg_hbm.at[e], sg_ref, sem.at[0]),
                pltpu.make_async_copy(wu_hbm.at[e], su_ref, sem.at[1]),
                pltpu.make_async_copy(wd_hbm.at[e], sd_ref, sem.at[2]))

    @pl.when(i == 0)
    def _():
        for cp in weight_copies(te_ref[0]):
            cp.start()

    valid = i < nv_ref[0]

    @pl.when(valid & (first_ref[i] == 1))
    def _():
        cps = weight_copies(te_ref[i])
        for cp, stage, dst in zip(cps, (sg_ref, su_ref, sd_ref), (wg_ref, wu_ref, wd_ref)):
            cp.wait()
            dst[...] = stage[...].astype(BF16)

        @pl.when(nxt_ref[i] >= 0)
        def _():
            for cp in weight_copies(nxt_ref[i]):
                cp.start()

    @pl.when(valid)
    def _():
        x = x_ref[...]
        gt = jnp.dot(x, wg_ref[...], preferred_element_type=F32)
        up = jnp.dot(x, wu_ref[...], preferred_element_type=F32)
        hmid = (gt * jax.nn.sigmoid(gt) * up).astype(BF16)
        y_ref[...] = jnp.dot(hmid, wd_ref[...], preferred_element_type=F32)

    @pl.when(jnp.logical_not(valid))
    def _():
        y_ref[...] = jnp.zeros_like(y_ref)


def _moe_ffn(xs, tile_expert, tile_index, tile_first, tile_next, n_valid, wg, wu, wd):
    rows, d = xs.shape
    n_tiles = rows // MOE_TILE
    hbm = pl.BlockSpec(memory_space=pl.ANY)
    grid_spec = pltpu.PrefetchScalarGridSpec(
        num_scalar_prefetch=5,
        grid=(n_tiles,),
        in_specs=[pl.BlockSpec((MOE_TILE, d), lambda i, te, tx, fr, nx, nv: (tx[i], 0)), hbm, hbm, hbm],
        out_specs=pl.BlockSpec((MOE_TILE, d), lambda i, te, tx, fr, nx, nv: (i, 0)),
        scratch_shapes=[pltpu.VMEM((d, D_EXPERT), F32), pltpu.VMEM((d, D_EXPERT), F32),
                        pltpu.VMEM((D_EXPERT, d), F32),
                        pltpu.VMEM((d, D_EXPERT), BF16), pltpu.VMEM((d, D_EXPERT), BF16),
                        pltpu.VMEM((D_EXPERT, d), BF16),
                        pltpu.SemaphoreType.DMA((3,))],
    )
    return pl.pallas_call(
        _moe_kernel,
        grid_spec=grid_spec,
        out_shape=jax.ShapeDtypeStruct((rows, d), F32),
        compiler_params=_cparams("arbitrary"),
        name="moe_ffn",
    )(tile_expert, tile_index, tile_first, tile_next, n_valid, xs, wg, wu, wd)


def _route(logits):
    lg = logits[:, :N_GROUPS]
    pg = jax.nn.softmax(lg, axis=-1)
    grp = jnp.argmax(lg, axis=-1)
    p_grp = jnp.take_along_axis(pg, grp[:, None], axis=-1)
    le = logits[:, N_GROUPS:N_GROUPS + N_EXPERTS].reshape(-1, N_GROUPS, EXPERTS_PER_GROUP)
    le = jnp.take_along_axis(le, grp[:, None, None], axis=1)[:, 0]
    top_p, top_i = lax.top_k(jax.nn.softmax(le, axis=-1), TOP_K_FINE)
    gate = p_grp * top_p / jnp.sum(top_p, axis=-1, keepdims=True)
    eid = grp[:, None] * EXPERTS_PER_GROUP + top_i
    return eid.astype(jnp.int32), gate


RANK_BLOCK = 256


def _moe(h2, logits, wg, wu, wd):
    n, d = h2.shape
    eid, gate = _route(logits)
    m = n * TOP_K_FINE
    eid_f = eid.reshape(-1)
    tok = jnp.repeat(jnp.arange(n, dtype=jnp.int32), TOP_K_FINE)
    experts = jnp.arange(N_EXPERTS, dtype=jnp.int32)
    onehot = eid_f[:, None] == experts[None, :]
    nb = m // RANK_BLOCK
    tri = (jnp.arange(RANK_BLOCK)[:, None] >= jnp.arange(RANK_BLOCK)[None, :]).astype(BF16)
    within = jnp.einsum('ij,bjk->bik', tri, onehot.astype(BF16).reshape(nb, RANK_BLOCK, N_EXPERTS),
                        preferred_element_type=F32)
    block_tot = within[:, -1, :]
    block_off = jnp.cumsum(block_tot, axis=0) - block_tot
    incl = (within + block_off[:, None, :]).reshape(m, N_EXPERTS)
    counts = (block_off[-1] + block_tot[-1]).astype(jnp.int32)
    padded = (counts + MOE_TILE - 1) // MOE_TILE * MOE_TILE
    pend = jnp.cumsum(padded)
    pstart = pend - padded
    start = jnp.cumsum(counts) - counts
    dest = jnp.sum(jnp.where(onehot, incl - 1.0 + pstart.astype(F32)[None, :], 0.0), axis=-1).astype(jnp.int32)
    pos = dest.reshape(n, TOP_K_FINE)

    n_tiles = -(-(m + N_EXPERTS * (MOE_TILE - 1)) // MOE_TILE)
    cap = n_tiles * MOE_TILE
    buf_tok = jnp.zeros((cap,), jnp.int32).at[dest].set(tok, unique_indices=True)
    xs = h2[buf_tok]

    n_valid = (pend[-1] // MOE_TILE).astype(jnp.int32)
    tiles = jnp.arange(n_tiles, dtype=jnp.int32)
    tile_index = jnp.minimum(tiles, n_valid - 1)
    tile_expert = jnp.minimum(jnp.sum((tile_index * MOE_TILE)[:, None] >= pend[None, :], axis=-1),
                              N_EXPERTS - 1).astype(jnp.int32)
    prev_expert = jnp.concatenate([jnp.full((1,), -1, jnp.int32), tile_expert[:-1]])
    tile_first = (tile_expert != prev_expert).astype(jnp.int32)
    next_tile = pend[tile_expert] // MOE_TILE
    tile_next = jnp.where(next_tile < n_valid, tile_expert[jnp.minimum(next_tile, n_tiles - 1)], -1)
    ys = _moe_ffn(xs, tile_expert, tile_index, tile_first, tile_next.astype(jnp.int32), n_valid.reshape(1),
                  wg, wu, wd)
    return ys, pos, gate


def _combine(x2, ys, pos, gate):
    return x2 + ys[pos[:, 0]] * gate[:, 0:1] + ys[pos[:, 1]] * gate[:, 1:2]


def _pad_cols(w, width):
    return jnp.pad(w, ((0, 0), (0, width - w.shape[1])))


def _pad_rows(w, height):
    return jnp.pad(w, ((0, height - w.shape[0]), (0, 0)))


def _prepare_params(w_in, tmix_mu, decay_w0, decay_up, iclr_a0, iclr_up, gate_up, k_k, k_a, r_k,
                    gn_w, gn_b, w_out, router_group, router_expert):
    c_att = 3 * D_ATT
    c3 = c_att + 3 * D_RWKV
    c4, c5 = c3 + LORA_DECAY, c3 + LORA_DECAY + LORA_ICLR
    w_rw = jnp.concatenate([w_in[:, c_att:c3], _pad_cols(w_in[:, c3:c4], LORA_PAD),
                            _pad_cols(w_in[:, c4:c5], LORA_PAD), w_in[:, c5:]], axis=1)
    mu = tmix_mu.reshape(1, -1)
    o3 = 3 * D_RWKV
    mu = jnp.concatenate([mu[:, :o3], _pad_cols(mu[:, o3:o3 + LORA_DECAY], LORA_PAD),
                          _pad_cols(mu[:, o3 + LORA_DECAY:o3 + LORA_DECAY + LORA_ICLR], LORA_PAD),
                          mu[:, o3 + LORA_DECAY + LORA_ICLR:]], axis=1)
    router = _pad_cols(jnp.concatenate([router_group, router_expert], axis=1), ROUTER_PAD)
    rt_hi = router.astype(BF16)
    rt_lo = (router - rt_hi.astype(F32)).astype(BF16)
    row = lambda z: z.reshape(1, -1)
    return dict(
        w_att=w_in[:, :c_att].astype(BF16), w_rw=w_rw.astype(BF16), mu=mu,
        w0=row(decay_w0), decay_up=_pad_rows(decay_up, LORA_PAD), a0=row(iclr_a0),
        iclr_up=_pad_rows(iclr_up, LORA_PAD), gate_up=gate_up, k_k=row(k_k), k_a=row(k_a),
        r_k=row(r_k), gn_w=row(gn_w), gn_b=row(gn_b),
        wo_att=w_out[:D_ATT].astype(BF16), wo_rw=w_out[D_ATT:].astype(BF16),
        rt1=jnp.concatenate([rt_hi, rt_lo], axis=1), rt2=rt_hi)


def _mixers(x, pos, shift_row, wkv0, k_ctx, v_ctx, prm, norm1_g, q_norm_g, k_norm_g, bm, bt):
    b, t, d = x.shape
    rows = b * t
    x2d = x.reshape(rows, d)
    g1 = norm1_g.reshape(1, d)
    proj_att = _norm_proj(x2d, g1, prm["w_att"], normalize=True, bm=bm, bn=prm["w_att"].shape[1] // 2,
                          name="proj_att").reshape(b, t, 3 * D_ATT)
    proj_rw = _norm_proj(x2d, g1, prm["w_rw"], normalize=True, bm=bm, bn=D_RW_PROJ // 2,
                         name="proj_rw").reshape(b, t, D_RW_PROJ)
    if shift_row is None:
        shift_proj = jnp.zeros((b, 1, D_RW_PROJ), F32)
    else:
        shift_proj = _norm_proj(shift_row, g1, prm["w_rw"], normalize=False, bm=b, bn=D_RW_PROJ // 2,
                                name="proj_shift").reshape(b, 1, D_RW_PROJ)

    q, k = _qk_prep(proj_att, pos, q_norm_g, k_norm_g, b, t, bt)
    v = proj_att[:, :, 2 * D_ATT:]
    if k_ctx is None:
        o_att = _attn_prompt(q, k, proj_att, b, t)
    else:
        o_att = _attn_sample(q, k, v, k_ctx, v_ctx)

    r, kr, vr, kkn, beta, logd, g = _rwkv_prep(proj_rw, shift_proj, prm, b, t, bt)
    s0 = _state_to_pairs(wkv0)
    tp = -(-t // CHUNK) * CHUNK
    if tp != t:
        padt = lambda z: jnp.pad(z, ((0, 0), (0, tp - t), (0, 0)))
        r, kr, vr, kkn, beta, logd, g = (padt(z) for z in (r, kr, vr, kkn, beta, logd, g))
    o_rwkv, s_new = _rwkv_chunk(r, kr, vr, kkn, beta, logd, g, s0, prm, b, tp)
    o_rwkv = o_rwkv[:, :t]
    return (o_att.reshape(rows, D_ATT), o_rwkv.reshape(rows, D_RWKV), k, v, _pairs_to_state(s_new))


def kernel(x_prompt, x_sample, cache_k_win, cache_v_win, state_wkv, state_shift, norm1_g, w_in, q_norm_g,
           k_norm_g, tmix_mu, decay_w0, decay_up, iclr_a0, iclr_up, gate_up, k_k, k_a, r_k, gn_w, gn_b,
           w_out, norm2_g, router_group, router_expert, moe_w_gate, moe_w_up, moe_w_down):
    depth = w_in.shape[0]
    assert depth == 1
    l = 0
    b_p, s_p, d = x_prompt.shape
    b_s, t_s, _ = x_sample.shape
    n_ctx = cache_k_win.shape[2]
    prm = _prepare_params(w_in[l], tmix_mu[l], decay_w0[l], decay_up[l], iclr_a0[l], iclr_up[l],
                          gate_up[l], k_k[l], k_a[l], r_k[l], gn_w[l], gn_b[l], w_out[l],
                          router_group[l], router_expert[l])
    g2 = norm2_g[l].reshape(1, d)

    pos_p = jnp.arange(s_p, dtype=F32)
    pos_s = PAST_LEN + jnp.arange(t_s, dtype=F32)
    wkv_zero = jnp.zeros((b_p, N_HEADS_RWKV, HEAD_DIM, HEAD_DIM), F32)

    oa_p, or_p, k_p, v_p, wkv_p = _mixers(x_prompt, pos_p, None, wkv_zero, None, None, prm,
                                          norm1_g[l], q_norm_g[l], k_norm_g[l], bm=512, bt=256)
    ck = cache_k_win[l].reshape(b_s, n_ctx, D_ATT)
    cv = cache_v_win[l].reshape(b_s, n_ctx, D_ATT)
    oa_s, or_s, k_s, v_s, wkv_s = _mixers(x_sample, pos_s, state_shift[l], state_wkv[l], ck, cv, prm,
                                          norm1_g[l], q_norm_g[l], k_norm_g[l], bm=b_s * t_s, bt=t_s)

    n_p, n_s = b_p * s_p, b_s * t_s
    x2_p, h2_p, lg_p = _out_router(x_prompt.reshape(n_p, d), oa_p, or_p, prm["wo_att"], prm["wo_rw"], g2,
                                   prm["rt1"], prm["rt2"], bm=512)
    x2_s, h2_s, lg_s = _out_router(x_sample.reshape(n_s, d), oa_s, or_s, prm["wo_att"], prm["wo_rw"], g2,
                                   prm["rt1"], prm["rt2"], bm=n_s)

    h2 = jnp.concatenate([h2_p, h2_s], axis=0)
    lg = jnp.concatenate([lg_p, lg_s], axis=0)
    ys, pos, gate = _moe(h2, lg, moe_w_gate[l], moe_w_up[l], moe_w_down[l])
    y_p = _combine(x2_p, ys, pos[:n_p], gate[:n_p]).reshape(b_p, s_p, d)
    y_s = _combine(x2_s, ys, pos[n_p:], gate[n_p:]).reshape(b_s, t_s, d)

    last = jnp.concatenate([x_prompt[:, -1], x_sample[:, -1]], axis=0)
    shift = _rms_rows(last, norm1_g[l].reshape(1, d))

    keep = min(MAX_WINDOW, s_p)
    heads = lambda z, b, t: z.reshape(1, b, t, N_HEADS_ATT, HEAD_DIM)
    return (y_p, y_s,
            heads(k_p[:, s_p - keep:], b_p, keep), heads(v_p[:, s_p - keep:], b_p, keep),
            wkv_p[None], shift[:b_p][None],
            heads(k_s, b_s, t_s), heads(v_s, b_s, t_s), wkv_s[None], shift[b_p:][None])
```

```python
import functools

import jax
import jax.numpy as jnp
from jax import lax
from jax.experimental import pallas as pl
from jax.experimental.pallas import tpu as pltpu

D_MODEL = 2048
HEAD_DIM = 64
N_HEADS_ATT = 12
N_HEADS_RWKV = 20
D_ATT = N_HEADS_ATT * HEAD_DIM
D_RWKV = N_HEADS_RWKV * HEAD_DIM
DIL_BRANCHES = ((128, 1), (512, 4), (2048, 16))
MAX_WINDOW = 2048
BAND_BLOCK = 128
ROT_DIM = HEAD_DIM // 4
ROPE_THETA = 500000.0
LORA_DECAY = 96
LORA_ICLR = 96
LORA_GATE = 256
GN_EPS = 64e-5
RMS_EPS = 1e-6
N_GROUPS = 8
EXPERTS_PER_GROUP = 8
N_EXPERTS = N_GROUPS * EXPERTS_PER_GROUP
TOP_K_FINE = 2
D_EXPERT = 1024
PAST_LEN = 16384

LANES = 128
SUBLANES = 8
VMEM_LIMIT_BYTES = 56 * 1024 * 1024

LORA_PAD = LANES
D_RW_PROJ = 3 * D_RWKV + 2 * LORA_PAD + LORA_GATE
N_PAIRS = D_RWKV // LANES
CHUNK = 64
ROUTER_PAD = LANES
MOE_TILE = 256
ATTN_GROUP = 4

F32 = jnp.float32
BF16 = jnp.bfloat16
HIGHEST = lax.Precision.HIGHEST


def _cparams(*sem):
    return pltpu.CompilerParams(dimension_semantics=("arbitrary",) * len(sem),
                                vmem_limit_bytes=VMEM_LIMIT_BYTES)


def _dot(a, b):
    return jnp.dot(a.astype(BF16), b.astype(BF16), preferred_element_type=F32)


def _dot_nt(a, b):
    return lax.dot_general(a.astype(BF16), b.astype(BF16), (((1,), (1,)), ((), ())),
                           preferred_element_type=F32)


def _dot_tn(a, b):
    return lax.dot_general(a.astype(BF16), b.astype(BF16), (((0,), (0,)), ((), ())),
                           preferred_element_type=F32)


def _split2(x):
    hi = x.astype(BF16)
    lo = (x - hi.astype(F32)).astype(BF16)
    return hi, lo


def _split3(x):
    hi = x.astype(BF16)
    r = x - hi.astype(F32)
    mid = r.astype(BF16)
    lo = (r - mid.astype(F32)).astype(BF16)
    return hi, mid, lo


def _dot_exact_rhs(x, e_bf16):
    hi, lo = _split2(x)
    return (jnp.dot(hi, e_bf16, preferred_element_type=F32)
            + jnp.dot(lo, e_bf16, preferred_element_type=F32))


def _head_sum_matrix(width=LANES):
    r = lax.broadcasted_iota(jnp.int32, (width, width), 0) // HEAD_DIM
    c = lax.broadcasted_iota(jnp.int32, (width, width), 1) // HEAD_DIM
    return (r == c).astype(BF16)


def _proj_kernel(x_ref, g_ref, w_ref, o_ref, h_ref, *, normalize):
    @pl.when(pl.program_id(1) == 0)
    def _():
        x = x_ref[...]
        if normalize:
            ms = jnp.mean(x * x, axis=-1, keepdims=True)
            x = x * lax.rsqrt(ms + RMS_EPS) * g_ref[...]
        h_ref[...] = x.astype(BF16)

    o_ref[...] = jnp.dot(h_ref[...], w_ref[...], preferred_element_type=F32)


def _norm_proj(x, g, w_bf16, *, normalize, bm, bn, name):
    m, d = x.shape
    n = w_bf16.shape[1]
    return pl.pallas_call(
        functools.partial(_proj_kernel, normalize=normalize),
        grid=(m // bm, n // bn),
        in_specs=[pl.BlockSpec((bm, d), lambda i, j: (i, 0)),
                  pl.BlockSpec((1, d), lambda i, j: (0, 0)),
                  pl.BlockSpec((d, bn), lambda i, j: (0, j))],
        out_specs=pl.BlockSpec((bm, bn), lambda i, j: (i, j)),
        out_shape=jax.ShapeDtypeStruct((m, n), F32),
        scratch_shapes=[pltpu.VMEM((bm, d), BF16)],
        compiler_params=_cparams("parallel", "arbitrary"),
        name=name,
    )(x, g, w_bf16)


def _proj_att_kernel(x_ref, g_ref, w_ref, qk_ref, v_ref, h_ref):
    j = pl.program_id(1)

    @pl.when(j == 0)
    def _():
        x = x_ref[...]
        ms = jnp.mean(x * x, axis=-1, keepdims=True)
        h_ref[...] = (x * lax.rsqrt(ms + RMS_EPS) * g_ref[...]).astype(BF16)

    y = jnp.dot(h_ref[...], w_ref[...], preferred_element_type=F32)

    @pl.when(j < 2)
    def _():
        qk_ref[...] = y

    @pl.when(j == 2)
    def _():
        v_ref[...] = y


def _norm_proj_att(x, g, w_bf16, *, bm):
    m, d = x.shape
    return pl.pallas_call(
        _proj_att_kernel,
        grid=(m // bm, 3),
        in_specs=[pl.BlockSpec((bm, d), lambda i, j: (i, 0)),
                  pl.BlockSpec((1, d), lambda i, j: (0, 0)),
                  pl.BlockSpec((d, D_ATT), lambda i, j: (0, j))],
        out_specs=[pl.BlockSpec((bm, D_ATT), lambda i, j: (i, jnp.minimum(j, 1))),
                   pl.BlockSpec((bm, D_ATT), lambda i, j: (i, 0))],
        out_shape=[jax.ShapeDtypeStruct((m, 2 * D_ATT), F32), jax.ShapeDtypeStruct((m, D_ATT), F32)],
        scratch_shapes=[pltpu.VMEM((bm, d), BF16)],
        compiler_params=_cparams("arbitrary", "arbitrary"),
        name="proj_att",
    )(x, g, w_bf16)


def _rms_rows_kernel(x_ref, g_ref, o_ref):
    x = x_ref[...]
    ms = jnp.mean(x * x, axis=-1, keepdims=True)
    o_ref[...] = x * lax.rsqrt(ms + RMS_EPS) * g_ref[...]


def _rms_rows(x, g):
    return pl.pallas_call(
        _rms_rows_kernel,
        out_shape=jax.ShapeDtypeStruct(x.shape, F32),
        name="rms_rows",
    )(x, g)


def _qk_prep_kernel(q_ref, k_ref, cos_ref, sa_ref, sb_ref, qg_ref, kg_ref, qo_ref, ko_ref):
    e = _head_sum_matrix()
    cos = cos_ref[...]
    sa = sa_ref[...]
    sb = sb_ref[...]

    def norm_rope(x, g):
        ss = _dot_exact_rhs(x * x, e)
        y = x * lax.rsqrt(ss * (1.0 / HEAD_DIM) + RMS_EPS) * g
        nxt = pltpu.roll(y, LANES - ROT_DIM // 2, axis=1)
        prv = pltpu.roll(y, ROT_DIM // 2, axis=1)
        return y * cos + nxt * sa + prv * sb

    for c in range(D_ATT // LANES):
        sl = slice(c * LANES, (c + 1) * LANES)
        qo_ref[0, :, sl] = norm_rope(q_ref[0, :, sl], qg_ref[...]) * (HEAD_DIM ** -0.5)
        ko_ref[0, :, sl] = norm_rope(k_ref[0, :, sl], kg_ref[...])


def _rope_tables(pos):
    half = ROT_DIM // 2
    inv_freq = ROPE_THETA ** (-2.0 * jnp.arange(half, dtype=F32) / ROT_DIM)
    ang = pos[:, None] * inv_freq[None, :]
    cos, sin = jnp.cos(ang), jnp.sin(ang)
    t = pos.shape[0]
    one = jnp.ones((t, HEAD_DIM - ROT_DIM), F32)
    zero = jnp.zeros((t, HEAD_DIM - ROT_DIM), F32)
    zh = jnp.zeros((t, half), F32)
    cos_h = jnp.concatenate([cos, cos, one], axis=1)
    sa_h = jnp.concatenate([-sin, zh, zero], axis=1)
    sb_h = jnp.concatenate([zh, sin, zero], axis=1)
    rep = LANES // HEAD_DIM
    return tuple(jnp.tile(z, (1, rep)) for z in (cos_h, sa_h, sb_h))


def _qk_prep(proj_att, pos, q_norm_g, k_norm_g, b, t, bt):
    cos, sa, sb = _rope_tables(pos)
    rep = LANES // HEAD_DIM
    qg = jnp.tile(q_norm_g.reshape(1, HEAD_DIM), (1, rep))
    kg = jnp.tile(k_norm_g.reshape(1, HEAD_DIM), (1, rep))
    tab = pl.BlockSpec((bt, LANES), lambda i, j: (j, 0))
    gsp = pl.BlockSpec((1, LANES), lambda i, j: (0, 0))
    return pl.pallas_call(
        _qk_prep_kernel,
        grid=(b, t // bt),
        in_specs=[pl.BlockSpec((1, bt, D_ATT), lambda i, j: (i, j, 0)),
                  pl.BlockSpec((1, bt, D_ATT), lambda i, j: (i, j, 1)),
                  tab, tab, tab, gsp, gsp],
        out_specs=[pl.BlockSpec((1, bt, D_ATT), lambda i, j: (i, j, 0)),
                   pl.BlockSpec((1, bt, D_ATT), lambda i, j: (i, j, 0))],
        out_shape=[jax.ShapeDtypeStruct((b, t, D_ATT), F32)] * 2,
        compiler_params=_cparams("parallel", "arbitrary"),
        name="qk_prep",
    )(proj_att, proj_att, cos, sa, sb, qg, kg)


def _attn_prompt_kernel(q_ref, k_ref, v_ref, o_ref, acc_ref, m_ref, *, seq):
    blk = BAND_BLOCK
    lane = lax.broadcasted_iota(jnp.int32, (1, LANES), 1)
    lo = lane < HEAD_DIM
    n_units = seq // blk
    qi = lax.broadcasted_iota(jnp.int32, (2 * blk, 1), 0) % blk

    for g, (window, dil) in enumerate(DIL_BRANCHES):
        n_back = window // dil
        n_blk = seq // dil // blk
        with_prev = n_blk > 1
        nk = 2 * blk if with_prev else blk
        kj = lax.broadcasted_iota(jnp.int32, (1, nk), 1)
        dist = qi + (blk if with_prev else 0) - kj
        in_band = (dist >= 0) & (dist <= n_back)

        def group(gi, carry, g=g, dil=dil, with_prev=with_prev, in_band=in_band, kj=kj):
            units = []
            for j in range(ATTN_GROUP):
                u = gi * ATTN_GROUP + j
                r = u % dil
                n = u // dil
                start = r + n * (blk * dil)
                rows = pl.ds(start, blk, stride=dil)
                q = q_ref[0, rows, :]
                k = k_ref[0, rows, :]
                v = v_ref[0, rows, :]
                valid = in_band
                if with_prev:
                    pstart = jnp.maximum(start - blk * dil, r)
                    prows = pl.ds(pstart, blk, stride=dil)
                    k = jnp.concatenate([k_ref[0, prows, :], k], axis=0)
                    v = jnp.concatenate([v_ref[0, prows, :], v], axis=0)
                    valid = valid & ((kj >= blk) | (n > 0))
                qq = jnp.concatenate([jnp.where(lo, q, 0.0), jnp.where(lo, 0.0, q)], axis=0)
                units.append((rows, valid, qq.astype(BF16), k.astype(BF16), v))
            scores = [_dot_nt(qq, k) for _, _, qq, k, _ in units]
            probs = []
            for (rows, valid, _, _, _), s in zip(units, scores):
                s = jnp.where(valid, s, -jnp.inf)
                m = jnp.max(s, axis=-1, keepdims=True)
                probs.append((m, jnp.exp(s - m).astype(BF16)))
            accs = [(_dot(p[:blk], jnp.where(lo, v, 1.0)), _dot(p[blk:], jnp.where(lo, 1.0, v)))
                    for (_, _, _, _, v), (_, p) in zip(units, probs)]
            for (rows, _, _, _, _), (m, _), (a0, a1) in zip(units, probs, accs):
                acc_ref[g, 0, rows, :] = a0
                acc_ref[g, 1, rows, :] = a1
                m_ref[g, 0, rows, :] = jnp.broadcast_to(m[:blk], (blk, LANES))
                m_ref[g, 1, rows, :] = jnp.broadcast_to(m[blk:], (blk, LANES))
            return carry

        lax.fori_loop(0, n_units // ATTN_GROUP, group, 0)

    mb = 2 * blk

    def merge(i, carry):
        rows = pl.ds(pl.multiple_of(i * mb, mb), mb)
        out = None
        for e in range(LANES // HEAD_DIM):
            ms = [m_ref[g, e, rows, :] for g in range(len(DIL_BRANCHES))]
            mx = functools.reduce(jnp.maximum, ms)
            tot = None
            for g in range(len(DIL_BRANCHES)):
                term = jnp.exp(ms[g] - mx) * acc_ref[g, e, rows, :]
                tot = term if tot is None else tot + term
            den = pltpu.roll(tot, HEAD_DIM, axis=1)
            oe = tot / den
            out = oe if out is None else jnp.where((lane // HEAD_DIM) == e, oe, out)
        o_ref[0, rows, :] = out.astype(o_ref.dtype)
        return carry

    lax.fori_loop(0, seq // mb, merge, 0, unroll=2)


def _attn_prompt(q, k, v, b, t):
    n_pairs = D_ATT // LANES
    spec = pl.BlockSpec((1, t, LANES), lambda i, j: (i, 0, j))
    return pl.pallas_call(
        functools.partial(_attn_prompt_kernel, seq=t),
        grid=(b, n_pairs),
        in_specs=[spec, spec, spec],
        out_specs=spec,
        out_shape=jax.ShapeDtypeStruct((b, t, D_ATT), BF16),
        scratch_shapes=[pltpu.VMEM((len(DIL_BRANCHES), 2, t, LANES), F32),
                        pltpu.VMEM((len(DIL_BRANCHES), 2, t, LANES), F32)],
        compiler_params=_cparams("parallel", "parallel"),
        name="attn_prompt",
    )(q, k, v)


def _attn_sample_kernel(q_ref, kn_ref, vn_ref, ck_ref, cv_ref, o_ref, *, n_ctx, t_new):
    nh = N_HEADS_ATT
    nq = nh * t_new
    q = q_ref[0]
    qt = jnp.concatenate([q] * nh, axis=0)
    rowh = lax.broadcasted_iota(jnp.int32, (nq, D_ATT), 0) // t_new
    laneh = lax.broadcasted_iota(jnp.int32, (nq, D_ATT), 1) // HEAD_DIM
    own = rowh == laneh
    qbd = jnp.where(own, qt, 0.0)

    pad = LANES - t_new
    kn = jnp.concatenate([kn_ref[0], jnp.zeros((pad, D_ATT), F32)], axis=0)
    vn = jnp.concatenate([vn_ref[0], jnp.zeros((pad, D_ATT), F32)], axis=0)

    s_c = _dot_nt(qbd, ck_ref[0])
    s_n = _dot_nt(qbd, kn)

    def multiplicity(d):
        mult = jnp.zeros(d.shape, F32)
        for window, dil in DIL_BRANCHES:
            mult = mult + jnp.where((d >= 0) & (d <= window) & (d % dil == 0), 1.0, 0.0)
        return mult

    qi_c = lax.broadcasted_iota(jnp.int32, (nq, n_ctx), 0) % t_new
    d_c = n_ctx + qi_c - lax.broadcasted_iota(jnp.int32, (nq, n_ctx), 1)
    qi_n = lax.broadcasted_iota(jnp.int32, (nq, LANES), 0) % t_new
    col_n = lax.broadcasted_iota(jnp.int32, (nq, LANES), 1)
    d_n = jnp.where(col_n < t_new, qi_n - col_n, -1)
    mult_c = multiplicity(d_c)
    mult_n = multiplicity(d_n)

    s_c = jnp.where(mult_c > 0, s_c, -jnp.inf)
    s_n = jnp.where(mult_n > 0, s_n, -jnp.inf)
    mx = jnp.maximum(jnp.max(s_c, axis=-1, keepdims=True), jnp.max(s_n, axis=-1, keepdims=True))
    p_c = mult_c * jnp.exp(s_c - mx)
    p_n = mult_n * jnp.exp(s_n - mx)
    den = jnp.sum(p_c, axis=-1, keepdims=True) + jnp.sum(p_n, axis=-1, keepdims=True)
    o_full = (_dot(p_c, cv_ref[0]) + _dot(p_n, vn)) / den
    o_full = jnp.where(own, o_full, 0.0)
    o = o_full[0:t_new]
    for h in range(1, nh):
        o = o + o_full[h * t_new:(h + 1) * t_new]
    o_ref[0] = o.astype(o_ref.dtype)


def _attn_sample(q, k_new, v_new, cache_k, cache_v):
    b, t_new, _ = q.shape
    n_ctx = cache_k.shape[1]
    new = pl.BlockSpec((1, t_new, D_ATT), lambda i: (i, 0, 0))
    ctx = pl.BlockSpec((1, n_ctx, D_ATT), lambda i: (i, 0, 0))
    return pl.pallas_call(
        functools.partial(_attn_sample_kernel, n_ctx=n_ctx, t_new=t_new),
        grid=(b,),
        in_specs=[new, new, new, ctx, ctx],
        out_specs=new,
        out_shape=jax.ShapeDtypeStruct((b, t_new, D_ATT), BF16),
        compiler_params=_cparams("parallel"),
        name="attn_sample",
    )(q, k_new, v_new, cache_k, cache_v)


def _rwkv_prep_kernel(p_ref, sh_ref, mu_ref, w0_ref, dup_ref, a0_ref, iup_ref, gup_ref, kk_ref, ka_ref,
                      r_o, k_o, v_o, kkn_o, beta_o, logd_o, g_o, carry_ref):
    c1, c2, c3 = D_RWKV, 2 * D_RWKV, 3 * D_RWKV
    c4, c5 = c3 + LORA_PAD, c3 + 2 * LORA_PAD

    @pl.when(pl.program_id(1) == 0)
    def _():
        carry_ref[...] = sh_ref[0]

    cur = p_ref[0]
    bt = cur.shape[0]
    prev = pltpu.roll(cur, 1, axis=0)
    row = lax.broadcasted_iota(jnp.int32, (bt, 1), 0)
    prev = jnp.where(row == 0, carry_ref[...], prev)
    carry_ref[...] = cur[bt - 1:bt]
    xm = cur + (prev - cur) * mu_ref[...]

    r, k, v = xm[:, :c1], xm[:, c1:c2], xm[:, c2:c3]
    wd, ad, gd = xm[:, c3:c4], xm[:, c4:c5], xm[:, c5:]

    z = -(w0_ref[...] + _dot(jnp.tanh(wd), dup_ref[...]))
    softplus = jnp.maximum(z, 0.0) + jnp.log(1.0 + jnp.exp(-jnp.abs(z)))
    w = -softplus - 0.5
    logd_o[0] = -jnp.exp(w)
    a = jax.nn.sigmoid(a0_ref[...] + _dot(ad, iup_ref[...]))
    g_o[0] = _dot(jax.nn.sigmoid(gd), gup_ref[...])

    kk = k * kk_ref[...]
    e = _head_sum_matrix()
    for c in range(N_PAIRS):
        sl = slice(c * LANES, (c + 1) * LANES)
        kc = kk[:, sl]
        ss = _dot_exact_rhs(kc * kc, e)
        kn = kc * lax.rsqrt(jnp.maximum(ss, 1e-24))
        kkn_o[0, :, sl] = kn
        beta_o[0, :, sl] = kn * a[:, sl]
    r_o[0] = r
    k_o[0] = k * (1.0 + (a - 1.0) * ka_ref[...])
    v_o[0] = v


def _rwkv_prep(proj_rw, shift_proj, prm, b, t, bt):
    blk = pl.BlockSpec((1, bt, D_RW_PROJ), lambda i, j: (i, j, 0))
    full = lambda a: pl.BlockSpec(a.shape, lambda i, j: (0,) * a.ndim)
    out = pl.BlockSpec((1, bt, D_RWKV), lambda i, j: (i, j, 0))
    args = (prm["mu"], prm["w0"], prm["decay_up"], prm["a0"], prm["iclr_up"], prm["gate_up"],
            prm["k_k"], prm["k_a"])
    return pl.pallas_call(
        _rwkv_prep_kernel,
        grid=(b, t // bt),
        in_specs=[blk, pl.BlockSpec((1, 1, D_RW_PROJ), lambda i, j: (i, 0, 0))] + [full(a) for a in args],
        out_specs=[out] * 7,
        out_shape=[jax.ShapeDtypeStruct((b, t, D_RWKV), F32)] * 7,
        scratch_shapes=[pltpu.VMEM((1, D_RW_PROJ), F32)],
        compiler_params=_cparams("parallel", "arbitrary"),
        name="rwkv_prep",
    )(proj_rw, shift_proj, *args)


def _rwkv_chunk_kernel(r_ref, k_ref, v_ref, kkn_ref, beta_ref, logd_ref, g_ref, s0_ref,
                       rk_ref, gnw_ref, gnb_ref, o_ref, sout_ref, s_ref):
    c = CHUNK
    ci = pl.program_id(1)

    @pl.when(ci == 0)
    def _():
        s_ref[...] = s0_ref[0]

    tri = (lax.broadcasted_iota(jnp.int32, (c, c), 0) >= lax.broadcasted_iota(jnp.int32, (c, c), 1)).astype(BF16)
    logd = logd_ref[0]
    l1, l2, l3 = _split3(logd)
    cum = (jnp.dot(tri, l1, preferred_element_type=F32) + jnp.dot(tri, l2, preferred_element_type=F32)
           + jnp.dot(tri, l3, preferred_element_type=F32))
    total = cum[c - 1:c]
    e_in = jnp.exp(cum)
    e_ex = jnp.exp(cum - logd)
    e_neg = jnp.exp(-cum)
    e_end = jnp.exp(total - cum)
    p_end = jnp.exp(total)

    r = r_ref[0]
    k = k_ref[0]
    v = v_ref[0]
    beta = beta_ref[0]
    alpha_t = -kkn_ref[0] * e_ex
    r_t = r * e_in
    beta_t = beta * e_neg
    k_t = k * e_neg
    beta_e = beta * e_end
    k_e = k * e_end

    lane = lax.broadcasted_iota(jnp.int32, (1, LANES), 1)
    lo = lane < HEAD_DIM
    tt = lax.broadcasted_iota(jnp.int32, (c, 2 * c), 0)
    ss = lax.broadcasted_iota(jnp.int32, (c, 2 * c), 1) % c
    strict = ss < tt
    incl = ss <= tt
    same_head = (lax.broadcasted_iota(jnp.int32, (LANES, LANES), 0) // HEAD_DIM
                 == lax.broadcasted_iota(jnp.int32, (LANES, LANES), 1) // HEAD_DIM)

    def expand(x):
        return jnp.concatenate([jnp.where(lo, x, 0.0), jnp.where(lo, 0.0, x)], axis=0)

    pairs = range(N_PAIRS)
    sls = [slice(p * LANES, (p + 1) * LANES) for p in pairs]
    v_c = [v[:, sl].astype(BF16) for sl in sls]
    v_x = [expand(v[:, sl]).astype(BF16) for sl in sls]
    lc = [jnp.concatenate([alpha_t[:, sl], r_t[:, sl]], axis=0).astype(BF16) for sl in sls]
    rx = [jnp.concatenate([expand(beta_t[:, sl]), expand(k_t[:, sl])], axis=0).astype(BF16) for sl in sls]
    gl = [_dot_nt(lc[p], jnp.concatenate([rx[p], s_ref[p].astype(BF16)], axis=0)) for p in pairs]
    npow = [jnp.where(strict, gl[p][:c, :2 * c], 0.0).astype(BF16) for p in pairs]
    a_ak = [jnp.where(strict, gl[p][:c, 2 * c:4 * c], 0.0) for p in pairs]
    a_r = [jnp.concatenate([jnp.where(incl, gl[p][c:, :2 * c], 0.0),
                            jnp.where(incl, gl[p][c:, 2 * c:4 * c], 0.0)], axis=1).astype(BF16) for p in pairs]
    u = [gl[p][:c, 4 * c:] + _dot(a_ak[p], v_x[p]) for p in pairs]
    steps = c.bit_length() - 1
    for it in range(steps):
        if it + 1 < steps:
            rhs = [jnp.concatenate([expand(u[p]).astype(BF16), expand(npow[p])], axis=1) for p in pairs]
            both = [_dot(npow[p], rhs[p]) for p in pairs]
            u = [u[p] + both[p][:, :2 * c] for p in pairs]
            npow = [both[p][:, 2 * c:].astype(BF16) for p in pairs]
        else:
            u = [u[p] + _dot(npow[p], expand(u[p])) for p in pairs]
    u_b = [x.astype(BF16) for x in u]
    outs = [gl[p][c:, 4 * c:] + _dot(a_r[p], jnp.concatenate([expand(u_b[p]), v_x[p]], axis=0)) for p in pairs]
    for p in pairs:
        sl = sls[p]
        upd = _dot_tn(jnp.concatenate([u_b[p], v_c[p]], axis=0),
                      jnp.concatenate([beta_e[:, sl], k_e[:, sl]], axis=0))
        s_ref[p] = s_ref[p] * p_end[:, sl] + jnp.where(same_head, upd, 0.0)

    gw = 2 * LANES
    n_grp = D_RWKV // gw
    e = _head_sum_matrix(gw)
    o_all = jnp.concatenate([jnp.concatenate([outs[2 * q], outs[2 * q + 1]], axis=1) for q in range(n_grp)],
                            axis=0)

    def stack(x):
        return jnp.concatenate([x[:, q * gw:(q + 1) * gw] for q in range(n_grp)], axis=0)

    def stack_param(ref):
        x = ref[...]
        return jnp.concatenate([jnp.broadcast_to(x[:, q * gw:(q + 1) * gw], (c, gw)) for q in range(n_grp)],
                               axis=0)

    mean = _dot_exact_rhs(o_all, e) * (1.0 / HEAD_DIM)
    d = o_all - mean
    var = _dot_exact_rhs(d * d, e) * (1.0 / HEAD_DIM)
    y = d * lax.rsqrt(var + GN_EPS) * stack_param(gnw_ref) + stack_param(gnb_ref)
    bonus = _dot_exact_rhs(stack(r * k) * stack_param(rk_ref), e)
    y = (y + bonus * stack(v)) * stack(g_ref[0])
    for q in range(n_grp):
        o_ref[0, :, q * gw:(q + 1) * gw] = y[q * c:(q + 1) * c].astype(o_ref.dtype)

    @pl.when(ci == pl.num_programs(1) - 1)
    def _():
        sout_ref[0] = s_ref[...]


def _rwkv_chunk(r, k, v, kkn, beta, logd, g, s0, prm, b, t):
    blk = pl.BlockSpec((1, CHUNK, D_RWKV), lambda i, j: (i, j, 0))
    st = pl.BlockSpec((1, N_PAIRS, LANES, LANES), lambda i, j: (i, 0, 0, 0))
    par = pl.BlockSpec((1, D_RWKV), lambda i, j: (0, 0))
    return pl.pallas_call(
        _rwkv_chunk_kernel,
        grid=(b, t // CHUNK),
        in_specs=[blk] * 7 + [st, par, par, par],
        out_specs=[blk, st],
        out_shape=[jax.ShapeDtypeStruct((b, t, D_RWKV), BF16),
                   jax.ShapeDtypeStruct((b, N_PAIRS, LANES, LANES), F32)],
        scratch_shapes=[pltpu.VMEM((N_PAIRS, LANES, LANES), F32)],
        compiler_params=_cparams("parallel", "arbitrary"),
        name="rwkv_chunk",
    )(r, k, v, kkn, beta, logd, g, s0, prm["r_k"], prm["gn_w"], prm["gn_b"])


def _state_to_pairs(wkv):
    b = wkv.shape[0]
    w = wkv.reshape(b, N_PAIRS, 2, HEAD_DIM, HEAD_DIM)
    z = jnp.zeros((b, N_PAIRS, HEAD_DIM, HEAD_DIM), wkv.dtype)
    top = jnp.concatenate([w[:, :, 0], z], axis=-1)
    bot = jnp.concatenate([z, w[:, :, 1]], axis=-1)
    return jnp.concatenate([top, bot], axis=-2)


def _pairs_to_state(s):
    b = s.shape[0]
    h0 = s[:, :, :HEAD_DIM, :HEAD_DIM]
    h1 = s[:, :, HEAD_DIM:, HEAD_DIM:]
    return jnp.stack([h0, h1], axis=2).reshape(b, N_HEADS_RWKV, HEAD_DIM, HEAD_DIM)


def _out_router_kernel(x_ref, a_ref, r_ref, wa_ref, wr_ref, g2_ref, rt1_ref, rt2_ref,
                       x2_ref, h2_ref, lg_ref, *, n_real):
    i = pl.program_id(0)

    @pl.when(i < n_real)
    def _():
        x2 = (x_ref[...] + jnp.dot(a_ref[...], wa_ref[...], preferred_element_type=F32)
              + jnp.dot(r_ref[...], wr_ref[...], preferred_element_type=F32))
        x2_ref[...] = x2
        ms = jnp.mean(x2 * x2, axis=-1, keepdims=True)
        h2 = x2 * lax.rsqrt(ms + RMS_EPS) * g2_ref[...]
        hi, lo = _split2(h2)
        h2_ref[...] = h2
        t12 = jnp.dot(hi, rt1_ref[...], preferred_element_type=F32)
        t3 = jnp.dot(lo, rt2_ref[...], preferred_element_type=F32)
        lg_ref[...] = t12[:, :ROUTER_PAD] + t12[:, ROUTER_PAD:] + t3

    @pl.when(i >= n_real)
    def _():
        h2_ref[...] = jnp.zeros_like(h2_ref)
        lg_ref[...] = jnp.zeros_like(lg_ref)


def _out_router(x, o_att, o_rwkv, wa, wr, g2, rt1, rt2, bm, spare_blocks=0):
    m, d = x.shape
    n_real = m // bm
    last = n_real - 1
    row = lambda w: pl.BlockSpec((bm, w), lambda i: (jnp.minimum(i, last), 0))
    grow = lambda w: pl.BlockSpec((bm, w), lambda i: (i, 0))
    full = lambda a: pl.BlockSpec(a.shape, lambda i: (0,) * a.ndim)
    m_out = m + spare_blocks * bm
    return pl.pallas_call(
        functools.partial(_out_router_kernel, n_real=n_real),
        grid=(n_real + spare_blocks,),
        in_specs=[row(d), row(D_ATT), row(D_RWKV), full(wa), full(wr), full(g2), full(rt1), full(rt2)],
        out_specs=[row(d), grow(d), grow(ROUTER_PAD)],
        out_shape=[jax.ShapeDtypeStruct((m, d), F32), jax.ShapeDtypeStruct((m_out, d), F32),
                   jax.ShapeDtypeStruct((m_out, ROUTER_PAD), F32)],
        compiler_params=_cparams("arbitrary"),
        name="out_router",
    )(x, o_att, o_rwkv, wa, wr, g2, rt1, rt2)


def _moe_kernel(te_ref, first_ref, nxt_ref, nv_ref, tok_ref, tok_next_ref, h_hbm, wg_hbm, wu_hbm, wd_hbm,
                y_ref, x_ref, sg_ref, su_ref, sd_ref, wg_ref, wu_ref, wd_ref, sem, xsem):
    i = pl.program_id(0)
    n_valid = nv_ref[0]

    def start_rows(idx_ref, slot, unrolled=False):
        def body(r, carry):
            pltpu.make_async_copy(h_hbm.at[pl.ds(idx_ref[0, 0, r], 1)], x_ref.at[slot, pl.ds(r, 1)],
                                  xsem.at[slot]).start()
            return carry
        if unrolled:
            for r in range(MOE_TILE):
                body(r, 0)
        else:
            lax.fori_loop(0, MOE_TILE, body, 0, unroll=8)

    def wait_rows(slot):
        pltpu.make_async_copy(h_hbm.at[pl.ds(0, MOE_TILE)], x_ref.at[slot], xsem.at[slot]).wait()

    @pl.when(i == 0)
    def _():
        start_rows(tok_ref, 0)

    def weight_copies(e):
        return (pltpu.make_async_copy(wg_hbm.at[e], sg_ref, sem.at[0]),
                pltpu.make_async_copy(wu_hbm.at[e], su_ref, sem.at[1]),
                pltpu.make_async_copy(wd_hbm.at[e], sd_ref, sem.at[2]))

    @pl.when(i == 0)
    def _():
        for cp in weight_copies(te_ref[0]):
            cp.start()

    valid = i < n_valid

    @pl.when(valid & (first_ref[i] == 1))
    def _():
        cps = weight_copies(te_ref[i])
        for cp, stage, dst in zip(cps, (sg_ref, su_ref, sd_ref), (wg_ref, wu_ref, wd_ref)):
            cp.wait()
            dst[...] = stage[...].astype(BF16)

        @pl.when(nxt_ref[i] >= 0)
        def _():
            for cp in weight_copies(nxt_ref[i]):
                cp.start()

    @pl.when(valid)
    def _():
        slot = i % 2
        wait_rows(slot)
        start_rows(tok_next_ref, 1 - slot, unrolled=True)
        x = x_ref[slot].astype(BF16)
        gt = jnp.dot(x, wg_ref[...], preferred_element_type=F32)
        up = jnp.dot(x, wu_ref[...], preferred_element_type=F32)
        hmid = (gt * jax.nn.sigmoid(gt) * up).astype(BF16)
        y_ref[...] = jnp.dot(hmid, wd_ref[...], preferred_element_type=F32)

    @pl.when(jnp.logical_not(valid))
    def _():
        y_ref[...] = jnp.zeros_like(y_ref)

        @pl.when(i == n_valid)
        def _():
            wait_rows(i % 2)


def _moe_ffn(h_rows, buf_tok, tile_expert, tile_first, tile_next, n_valid, wg, wu, wd):
    d = h_rows.shape[1]
    n_tiles = buf_tok.shape[0] // MOE_TILE
    tok3 = buf_tok.reshape(n_tiles, 1, MOE_TILE)
    hbm = pl.BlockSpec(memory_space=pl.ANY)
    grid_spec = pltpu.PrefetchScalarGridSpec(
        num_scalar_prefetch=4,
        grid=(n_tiles,),
        in_specs=[pl.BlockSpec((1, 1, MOE_TILE), lambda i, te, fr, nx, nv: (i, 0, 0), memory_space=pltpu.SMEM),
                  pl.BlockSpec((1, 1, MOE_TILE), lambda i, te, fr, nx, nv: (jnp.minimum(i + 1, n_tiles - 1), 0, 0),
                               memory_space=pltpu.SMEM),
                  hbm, hbm, hbm, hbm],
        out_specs=pl.BlockSpec((MOE_TILE, d), lambda i, te, fr, nx, nv: (i, 0)),
        scratch_shapes=[pltpu.VMEM((2, MOE_TILE, d), F32),
                        pltpu.VMEM((d, D_EXPERT), F32), pltpu.VMEM((d, D_EXPERT), F32),
                        pltpu.VMEM((D_EXPERT, d), F32),
                        pltpu.VMEM((d, D_EXPERT), BF16), pltpu.VMEM((d, D_EXPERT), BF16),
                        pltpu.VMEM((D_EXPERT, d), BF16),
                        pltpu.SemaphoreType.DMA((3,)), pltpu.SemaphoreType.DMA((2,))],
    )
    return pl.pallas_call(
        _moe_kernel,
        grid_spec=grid_spec,
        out_shape=jax.ShapeDtypeStruct((n_tiles * MOE_TILE, d), F32),
        compiler_params=_cparams("arbitrary"),
        name="moe_ffn",
    )(tile_expert, tile_first, tile_next, n_valid, tok3, tok3, h_rows, wg, wu, wd)


def _route(logits):
    lg = logits[:, :N_GROUPS]
    pg = jax.nn.softmax(lg, axis=-1)
    grp = jnp.argmax(lg, axis=-1)
    p_grp = jnp.take_along_axis(pg, grp[:, None], axis=-1)
    le = logits[:, N_GROUPS:N_GROUPS + N_EXPERTS].reshape(-1, N_GROUPS, EXPERTS_PER_GROUP)
    le = jnp.take_along_axis(le, grp[:, None, None], axis=1)[:, 0]
    top_p, top_i = lax.top_k(jax.nn.softmax(le, axis=-1), TOP_K_FINE)
    gate = p_grp * top_p / jnp.sum(top_p, axis=-1, keepdims=True)
    eid = grp[:, None] * EXPERTS_PER_GROUP + top_i
    return eid.astype(jnp.int32), gate


RANK_BLOCK = 256


def _moe(h2, logits, wg, wu, wd):
    n = logits.shape[0]
    eid, gate = _route(logits)
    m = n * TOP_K_FINE
    eid_f = eid.reshape(-1)
    tok = jnp.repeat(jnp.arange(n, dtype=jnp.int32), TOP_K_FINE)
    experts = jnp.arange(N_EXPERTS, dtype=jnp.int32)
    onehot = eid_f[:, None] == experts[None, :]
    nb = m // RANK_BLOCK
    tri = (jnp.arange(RANK_BLOCK)[:, None] >= jnp.arange(RANK_BLOCK)[None, :]).astype(BF16)
    within = jnp.einsum('ij,bjk->bik', tri, onehot.astype(BF16).reshape(nb, RANK_BLOCK, N_EXPERTS),
                        preferred_element_type=F32)
    block_tot = within[:, -1, :]
    block_off = jnp.cumsum(block_tot, axis=0) - block_tot
    incl = (within + block_off[:, None, :]).reshape(m, N_EXPERTS)
    counts = (block_off[-1] + block_tot[-1]).astype(jnp.int32)
    padded = (counts + MOE_TILE - 1) // MOE_TILE * MOE_TILE
    pend = jnp.cumsum(padded)
    pstart = pend - padded
    start = jnp.cumsum(counts) - counts
    dest = jnp.sum(jnp.where(onehot, incl - 1.0 + pstart.astype(F32)[None, :], 0.0), axis=-1).astype(jnp.int32)
    pos = dest.reshape(n, TOP_K_FINE)

    n_tiles = -(-(m + N_EXPERTS * (MOE_TILE - 1)) // MOE_TILE) + 1
    cap = n_tiles * MOE_TILE
    buf_tok = jnp.zeros((cap,), jnp.int32).at[dest].set(tok, unique_indices=True)

    n_valid = (pend[-1] // MOE_TILE).astype(jnp.int32)
    tiles = jnp.arange(n_tiles, dtype=jnp.int32)
    tile_index = jnp.minimum(tiles, n_valid - 1)
    tile_expert = jnp.minimum(jnp.sum((tile_index * MOE_TILE)[:, None] >= pend[None, :], axis=-1),
                              N_EXPERTS - 1).astype(jnp.int32)
    prev_expert = jnp.concatenate([jnp.full((1,), -1, jnp.int32), tile_expert[:-1]])
    tile_first = (tile_expert != prev_expert).astype(jnp.int32)
    next_tile = pend[tile_expert] // MOE_TILE
    tile_next = jnp.where(next_tile < n_valid, tile_expert[jnp.minimum(next_tile, n_tiles - 1)], -1)
    ys = _moe_ffn(h2, buf_tok, tile_expert, tile_first, tile_next.astype(jnp.int32), n_valid.reshape(1),
                  wg, wu, wd)
    return ys, pos, gate


def _combine_kernel(pos_ref, pos_next_ref, x2_ref, gate_ref, ys_hbm, y_ref, rows_ref, sem, *, n_tiles, bt):
    i = pl.program_id(0)
    n_rows = TOP_K_FINE * bt

    def start_rows(idx_ref, slot, unrolled):
        def body(r, carry):
            pltpu.make_async_copy(ys_hbm.at[pl.ds(idx_ref[0, 0, r], 1)], rows_ref.at[slot, pl.ds(r, 1)],
                                  sem.at[slot]).start()
            return carry
        if unrolled:
            for r in range(n_rows):
                body(r, 0)
        else:
            lax.fori_loop(0, n_rows, body, 0, unroll=8)

    @pl.when(i == 0)
    def _():
        start_rows(pos_ref, 0, False)

    slot = i % 2

    @pl.when(i + 1 < n_tiles)
    def _():
        start_rows(pos_next_ref, 1 - slot, True)

    pltpu.make_async_copy(ys_hbm.at[pl.ds(0, n_rows)], rows_ref.at[slot], sem.at[slot]).wait()
    y_ref[...] = (x2_ref[...] + rows_ref[slot, 0:bt] * gate_ref[:, 0:1]
                  + rows_ref[slot, bt:n_rows] * gate_ref[:, 1:2])


def _combine(x2, ys, pos, gate, bt):
    n, d = x2.shape
    n_tiles = n // bt
    pos3 = pos.reshape(n_tiles, bt, TOP_K_FINE).transpose(0, 2, 1).reshape(n_tiles, 1, TOP_K_FINE * bt)
    smem = lambda f: pl.BlockSpec((1, 1, TOP_K_FINE * bt), f, memory_space=pltpu.SMEM)
    return pl.pallas_call(
        functools.partial(_combine_kernel, n_tiles=n_tiles, bt=bt),
        grid=(n_tiles,),
        in_specs=[smem(lambda i: (i, 0, 0)), smem(lambda i: (jnp.minimum(i + 1, n_tiles - 1), 0, 0)),
                  pl.BlockSpec((bt, d), lambda i: (i, 0)), pl.BlockSpec((bt, TOP_K_FINE), lambda i: (i, 0)),
                  pl.BlockSpec(memory_space=pl.ANY)],
        out_specs=pl.BlockSpec((bt, d), lambda i: (i, 0)),
        out_shape=jax.ShapeDtypeStruct((n, d), F32),
        scratch_shapes=[pltpu.VMEM((2, TOP_K_FINE * bt, d), F32), pltpu.SemaphoreType.DMA((2,))],
        compiler_params=_cparams("arbitrary"),
        name="moe_combine",
    )(pos3, pos3, x2, gate, ys)


def _pad_cols(w, width):
    return jnp.pad(w, ((0, 0), (0, width - w.shape[1])))


def _pad_rows(w, height):
    return jnp.pad(w, ((0, height - w.shape[0]), (0, 0)))


def _prepare_params(w_in, tmix_mu, decay_w0, decay_up, iclr_a0, iclr_up, gate_up, k_k, k_a, r_k,
                    gn_w, gn_b, w_out, router_group, router_expert):
    c_att = 3 * D_ATT
    c3 = c_att + 3 * D_RWKV
    c4, c5 = c3 + LORA_DECAY, c3 + LORA_DECAY + LORA_ICLR
    w_rw = jnp.concatenate([w_in[:, c_att:c3], _pad_cols(w_in[:, c3:c4], LORA_PAD),
                            _pad_cols(w_in[:, c4:c5], LORA_PAD), w_in[:, c5:]], axis=1)
    mu = tmix_mu.reshape(1, -1)
    o3 = 3 * D_RWKV
    mu = jnp.concatenate([mu[:, :o3], _pad_cols(mu[:, o3:o3 + LORA_DECAY], LORA_PAD),
                          _pad_cols(mu[:, o3 + LORA_DECAY:o3 + LORA_DECAY + LORA_ICLR], LORA_PAD),
                          mu[:, o3 + LORA_DECAY + LORA_ICLR:]], axis=1)
    router = _pad_cols(jnp.concatenate([router_group, router_expert], axis=1), ROUTER_PAD)
    rt_hi = router.astype(BF16)
    rt_lo = (router - rt_hi.astype(F32)).astype(BF16)
    row = lambda z: z.reshape(1, -1)
    return dict(
        w_att=w_in[:, :c_att].astype(BF16), w_rw=w_rw.astype(BF16), mu=mu,
        w0=row(decay_w0), decay_up=_pad_rows(decay_up, LORA_PAD), a0=row(iclr_a0),
        iclr_up=_pad_rows(iclr_up, LORA_PAD), gate_up=gate_up, k_k=row(k_k), k_a=row(k_a),
        r_k=row(r_k), gn_w=row(gn_w), gn_b=row(gn_b),
        wo_att=w_out[:D_ATT].astype(BF16), wo_rw=w_out[D_ATT:].astype(BF16),
        rt1=jnp.concatenate([rt_hi, rt_lo], axis=1), rt2=rt_hi)


def _mixers(x, pos, shift_row, wkv0, k_ctx, v_ctx, prm, norm1_g, q_norm_g, k_norm_g, bm, bt):
    b, t, d = x.shape
    rows = b * t
    x2d = x.reshape(rows, d)
    g1 = norm1_g.reshape(1, d)
    proj_qk, v = _norm_proj_att(x2d, g1, prm["w_att"], bm=bm)
    proj_qk = proj_qk.reshape(b, t, 2 * D_ATT)
    v = v.reshape(b, t, D_ATT)
    proj_rw = _norm_proj(x2d, g1, prm["w_rw"], normalize=True, bm=bm, bn=D_RW_PROJ // 2,
                         name="proj_rw").reshape(b, t, D_RW_PROJ)
    if shift_row is None:
        shift_proj = jnp.zeros((b, 1, D_RW_PROJ), F32)
    else:
        shift_proj = _norm_proj(shift_row, g1, prm["w_rw"], normalize=False, bm=b, bn=D_RW_PROJ // 2,
                                name="proj_shift").reshape(b, 1, D_RW_PROJ)

    q, k = _qk_prep(proj_qk, pos, q_norm_g, k_norm_g, b, t, bt)
    if k_ctx is None:
        o_att = _attn_prompt(q, k, v, b, t)
    else:
        o_att = _attn_sample(q, k, v, k_ctx, v_ctx)

    r, kr, vr, kkn, beta, logd, g = _rwkv_prep(proj_rw, shift_proj, prm, b, t, bt)
    s0 = _state_to_pairs(wkv0)
    tp = -(-t // CHUNK) * CHUNK
    if tp != t:
        padt = lambda z: jnp.pad(z, ((0, 0), (0, tp - t), (0, 0)))
        r, kr, vr, kkn, beta, logd, g = (padt(z) for z in (r, kr, vr, kkn, beta, logd, g))
    o_rwkv, s_new = _rwkv_chunk(r, kr, vr, kkn, beta, logd, g, s0, prm, b, tp)
    o_rwkv = o_rwkv[:, :t]
    return (o_att.reshape(rows, D_ATT), o_rwkv.reshape(rows, D_RWKV), k, v, _pairs_to_state(s_new))


def kernel(x_prompt, x_sample, cache_k_win, cache_v_win, state_wkv, state_shift, norm1_g, w_in, q_norm_g,
           k_norm_g, tmix_mu, decay_w0, decay_up, iclr_a0, iclr_up, gate_up, k_k, k_a, r_k, gn_w, gn_b,
           w_out, norm2_g, router_group, router_expert, moe_w_gate, moe_w_up, moe_w_down):
    depth = w_in.shape[0]
    assert depth == 1
    l = 0
    b_p, s_p, d = x_prompt.shape
    b_s, t_s, _ = x_sample.shape
    n_ctx = cache_k_win.shape[2]
    prm = _prepare_params(w_in[l], tmix_mu[l], decay_w0[l], decay_up[l], iclr_a0[l], iclr_up[l],
                          gate_up[l], k_k[l], k_a[l], r_k[l], gn_w[l], gn_b[l], w_out[l],
                          router_group[l], router_expert[l])
    g2 = norm2_g[l].reshape(1, d)

    pos_p = jnp.arange(s_p, dtype=F32)
    pos_s = PAST_LEN + jnp.arange(t_s, dtype=F32)
    wkv_zero = jnp.zeros((b_p, N_HEADS_RWKV, HEAD_DIM, HEAD_DIM), F32)

    oa_p, or_p, k_p, v_p, wkv_p = _mixers(x_prompt, pos_p, None, wkv_zero, None, None, prm,
                                          norm1_g[l], q_norm_g[l], k_norm_g[l], bm=512, bt=256)
    ck = cache_k_win[l].reshape(b_s, n_ctx, D_ATT)
    cv = cache_v_win[l].reshape(b_s, n_ctx, D_ATT)
    oa_s, or_s, k_s, v_s, wkv_s = _mixers(x_sample, pos_s, state_shift[l], state_wkv[l], ck, cv, prm,
                                          norm1_g[l], q_norm_g[l], k_norm_g[l], bm=b_s * t_s, bt=t_s)

    n_p, n_s = b_p * s_p, b_s * t_s
    x2_p, h2_p, lg_p = _out_router(x_prompt.reshape(n_p, d), oa_p, or_p, prm["wo_att"], prm["wo_rw"], g2,
                                   prm["rt1"], prm["rt2"], bm=512, spare_blocks=1)
    x2_s, h2_s, lg_s = _out_router(x_sample.reshape(n_s, d), oa_s, or_s, prm["wo_att"], prm["wo_rw"], g2,
                                   prm["rt1"], prm["rt2"], bm=n_s)

    h2 = lax.dynamic_update_slice(h2_p, h2_s, (n_p, 0))
    lg = lax.dynamic_update_slice(lg_p, lg_s, (n_p, 0))[:n_p + n_s]
    ys, pos, gate = _moe(h2, lg, moe_w_gate[l], moe_w_up[l], moe_w_down[l])
    y_p = _combine(x2_p, ys, pos[:n_p], gate[:n_p], bt=MOE_TILE).reshape(b_p, s_p, d)
    y_s = _combine(x2_s, ys, pos[n_p:], gate[n_p:], bt=n_s).reshape(b_s, t_s, d)

    last = jnp.concatenate([x_prompt[:, -1], x_sample[:, -1]], axis=0)
    shift = _rms_rows(last, norm1_g[l].reshape(1, d))

    keep = min(MAX_WINDOW, s_p)
    heads = lambda z, b, t: z.reshape(1, b, t, N_HEADS_ATT, HEAD_DIM)
    return (y_p, y_s,
            heads(k_p[:, s_p - keep:], b_p, keep), heads(v_p[:, s_p - keep:], b_p, keep),
            wkv_p[None], shift[:b_p][None],
            heads(k_s, b_s, t_s), heads(v_s, b_s, t_s), wkv_s[None], shift[b_p:][None])
```

```python
import functools

import jax
import jax.numpy as jnp
from jax import lax
from jax.experimental import pallas as pl
from jax.experimental.pallas import tpu as pltpu

D_MODEL = 2048
HEAD_DIM = 64
N_HEADS_ATT = 12
N_HEADS_RWKV = 20
D_ATT = N_HEADS_ATT * HEAD_DIM
D_RWKV = N_HEADS_RWKV * HEAD_DIM
DIL_BRANCHES = ((128, 1), (512, 4), (2048, 16))
MAX_WINDOW = 2048
BAND_BLOCK = 128
ROT_DIM = HEAD_DIM // 4
ROPE_THETA = 500000.0
LORA_DECAY = 96
LORA_ICLR = 96
LORA_GATE = 256
GN_EPS = 64e-5
RMS_EPS = 1e-6
N_GROUPS = 8
EXPERTS_PER_GROUP = 8
N_EXPERTS = N_GROUPS * EXPERTS_PER_GROUP
TOP_K_FINE = 2
D_EXPERT = 1024
PAST_LEN = 16384

LANES = 128
SUBLANES = 8
VMEM_LIMIT_BYTES = 56 * 1024 * 1024

LORA_PAD = LANES
D_RW_PROJ = 3 * D_RWKV + 2 * LORA_PAD + LORA_GATE
N_PAIRS = D_RWKV // LANES
CHUNK = 64
ROUTER_PAD = LANES
MOE_TILE = 256
ATTN_GROUP = 16

F32 = jnp.float32
BF16 = jnp.bfloat16
HIGHEST = lax.Precision.HIGHEST


def _cparams(*sem):
    return pltpu.CompilerParams(dimension_semantics=("arbitrary",) * len(sem),
                                vmem_limit_bytes=VMEM_LIMIT_BYTES)


def _dot(a, b):
    return jnp.dot(a.astype(BF16), b.astype(BF16), preferred_element_type=F32)


def _dot_nt(a, b):
    return lax.dot_general(a.astype(BF16), b.astype(BF16), (((1,), (1,)), ((), ())),
                           preferred_element_type=F32)


def _dot_tn(a, b):
    return lax.dot_general(a.astype(BF16), b.astype(BF16), (((0,), (0,)), ((), ())),
                           preferred_element_type=F32)


def _split2(x):
    hi = x.astype(BF16)
    lo = (x - hi.astype(F32)).astype(BF16)
    return hi, lo


def _split3(x):
    hi = x.astype(BF16)
    r = x - hi.astype(F32)
    mid = r.astype(BF16)
    lo = (r - mid.astype(F32)).astype(BF16)
    return hi, mid, lo


def _dot_exact_rhs(x, e_bf16):
    hi, lo = _split2(x)
    return (jnp.dot(hi, e_bf16, preferred_element_type=F32)
            + jnp.dot(lo, e_bf16, preferred_element_type=F32))


def _head_sum_matrix(width=LANES):
    r = lax.broadcasted_iota(jnp.int32, (width, width), 0) // HEAD_DIM
    c = lax.broadcasted_iota(jnp.int32, (width, width), 1) // HEAD_DIM
    return (r == c).astype(BF16)


def _proj_kernel(x_ref, g_ref, w_ref, o_ref, h_ref, *, normalize):
    @pl.when(pl.program_id(1) == 0)
    def _():
        x = x_ref[...]
        if normalize:
            ms = jnp.mean(x * x, axis=-1, keepdims=True)
            x = x * lax.rsqrt(ms + RMS_EPS) * g_ref[...]
        h_ref[...] = x.astype(BF16)

    o_ref[...] = jnp.dot(h_ref[...], w_ref[...], preferred_element_type=F32)


def _norm_proj(x, g, w_bf16, *, normalize, bm, bn, name):
    m, d = x.shape
    n = w_bf16.shape[1]
    return pl.pallas_call(
        functools.partial(_proj_kernel, normalize=normalize),
        grid=(m // bm, n // bn),
        in_specs=[pl.BlockSpec((bm, d), lambda i, j: (i, 0)),
                  pl.BlockSpec((1, d), lambda i, j: (0, 0)),
                  pl.BlockSpec((d, bn), lambda i, j: (0, j))],
        out_specs=pl.BlockSpec((bm, bn), lambda i, j: (i, j)),
        out_shape=jax.ShapeDtypeStruct((m, n), F32),
        scratch_shapes=[pltpu.VMEM((bm, d), BF16)],
        compiler_params=_cparams("parallel", "arbitrary"),
        name=name,
    )(x, g, w_bf16)


def _proj_att_kernel(x_ref, g_ref, w_ref, qk_ref, v_ref, h_ref):
    j = pl.program_id(1)

    @pl.when(j == 0)
    def _():
        x = x_ref[...]
        ms = jnp.mean(x * x, axis=-1, keepdims=True)
        h_ref[...] = (x * lax.rsqrt(ms + RMS_EPS) * g_ref[...]).astype(BF16)

    y = jnp.dot(h_ref[...], w_ref[...], preferred_element_type=F32)

    @pl.when(j < 2)
    def _():
        qk_ref[...] = y

    @pl.when(j == 2)
    def _():
        v_ref[...] = y


def _norm_proj_att(x, g, w_bf16, *, bm):
    m, d = x.shape
    return pl.pallas_call(
        _proj_att_kernel,
        grid=(m // bm, 3),
        in_specs=[pl.BlockSpec((bm, d), lambda i, j: (i, 0)),
                  pl.BlockSpec((1, d), lambda i, j: (0, 0)),
                  pl.BlockSpec((d, D_ATT), lambda i, j: (0, j))],
        out_specs=[pl.BlockSpec((bm, D_ATT), lambda i, j: (i, jnp.minimum(j, 1))),
                   pl.BlockSpec((bm, D_ATT), lambda i, j: (i, 0))],
        out_shape=[jax.ShapeDtypeStruct((m, 2 * D_ATT), F32), jax.ShapeDtypeStruct((m, D_ATT), F32)],
        scratch_shapes=[pltpu.VMEM((bm, d), BF16)],
        compiler_params=_cparams("arbitrary", "arbitrary"),
        name="proj_att",
    )(x, g, w_bf16)


def _rms_rows_kernel(x_ref, g_ref, o_ref):
    x = x_ref[...]
    ms = jnp.mean(x * x, axis=-1, keepdims=True)
    o_ref[...] = x * lax.rsqrt(ms + RMS_EPS) * g_ref[...]


def _rms_rows(x, g):
    return pl.pallas_call(
        _rms_rows_kernel,
        out_shape=jax.ShapeDtypeStruct(x.shape, F32),
        name="rms_rows",
    )(x, g)


def _qk_prep_kernel(q_ref, k_ref, cos_ref, sa_ref, sb_ref, qg_ref, kg_ref, qo_ref, ko_ref):
    e = _head_sum_matrix()
    cos = cos_ref[...]
    sa = sa_ref[...]
    sb = sb_ref[...]

    def norm_rope(x, g):
        ss = _dot_exact_rhs(x * x, e)
        y = x * lax.rsqrt(ss * (1.0 / HEAD_DIM) + RMS_EPS) * g
        nxt = pltpu.roll(y, LANES - ROT_DIM // 2, axis=1)
        prv = pltpu.roll(y, ROT_DIM // 2, axis=1)
        return y * cos + nxt * sa + prv * sb

    for c in range(D_ATT // LANES):
        sl = slice(c * LANES, (c + 1) * LANES)
        qo_ref[0, :, sl] = norm_rope(q_ref[0, :, sl], qg_ref[...]) * (HEAD_DIM ** -0.5)
        ko_ref[0, :, sl] = norm_rope(k_ref[0, :, sl], kg_ref[...])


def _rope_tables(pos):
    half = ROT_DIM // 2
    inv_freq = ROPE_THETA ** (-2.0 * jnp.arange(half, dtype=F32) / ROT_DIM)
    ang = pos[:, None] * inv_freq[None, :]
    cos, sin = jnp.cos(ang), jnp.sin(ang)
    t = pos.shape[0]
    one = jnp.ones((t, HEAD_DIM - ROT_DIM), F32)
    zero = jnp.zeros((t, HEAD_DIM - ROT_DIM), F32)
    zh = jnp.zeros((t, half), F32)
    cos_h = jnp.concatenate([cos, cos, one], axis=1)
    sa_h = jnp.concatenate([-sin, zh, zero], axis=1)
    sb_h = jnp.concatenate([zh, sin, zero], axis=1)
    rep = LANES // HEAD_DIM
    return tuple(jnp.tile(z, (1, rep)) for z in (cos_h, sa_h, sb_h))


def _qk_prep(proj_att, pos, q_norm_g, k_norm_g, b, t, bt):
    cos, sa, sb = _rope_tables(pos)
    rep = LANES // HEAD_DIM
    qg = jnp.tile(q_norm_g.reshape(1, HEAD_DIM), (1, rep))
    kg = jnp.tile(k_norm_g.reshape(1, HEAD_DIM), (1, rep))
    tab = pl.BlockSpec((bt, LANES), lambda i, j: (j, 0))
    gsp = pl.BlockSpec((1, LANES), lambda i, j: (0, 0))
    return pl.pallas_call(
        _qk_prep_kernel,
        grid=(b, t // bt),
        in_specs=[pl.BlockSpec((1, bt, D_ATT), lambda i, j: (i, j, 0)),
                  pl.BlockSpec((1, bt, D_ATT), lambda i, j: (i, j, 1)),
                  tab, tab, tab, gsp, gsp],
        out_specs=[pl.BlockSpec((1, bt, D_ATT), lambda i, j: (i, j, 0)),
                   pl.BlockSpec((1, bt, D_ATT), lambda i, j: (i, j, 0))],
        out_shape=[jax.ShapeDtypeStruct((b, t, D_ATT), F32)] * 2,
        compiler_params=_cparams("parallel", "arbitrary"),
        name="qk_prep",
    )(proj_att, proj_att, cos, sa, sb, qg, kg)


def _attn_prompt_kernel(q_ref, k_ref, v_ref, o_ref, acc_ref, m_ref, *, seq):
    blk = BAND_BLOCK
    lane = lax.broadcasted_iota(jnp.int32, (1, LANES), 1)
    lo = lane < HEAD_DIM
    n_units = seq // blk
    qi = lax.broadcasted_iota(jnp.int32, (2 * blk, 1), 0) % blk

    for g, (window, dil) in enumerate(DIL_BRANCHES):
        n_back = window // dil
        n_blk = seq // dil // blk
        with_prev = n_blk > 1
        nk = 2 * blk if with_prev else blk
        kj = lax.broadcasted_iota(jnp.int32, (1, nk), 1)
        dist = qi + (blk if with_prev else 0) - kj
        in_band = (dist >= 0) & (dist <= n_back)

        def group(gi, carry, g=g, dil=dil, with_prev=with_prev, in_band=in_band, kj=kj):
            units = []
            for j in range(ATTN_GROUP):
                u = gi * ATTN_GROUP + j
                r = u % dil
                n = u // dil
                start = r + n * (blk * dil)
                rows = pl.ds(start, blk, stride=dil)
                q = q_ref[0, rows, :]
                k = k_ref[0, rows, :]
                v = v_ref[0, rows, :]
                valid = in_band
                if with_prev:
                    pstart = jnp.maximum(start - blk * dil, r)
                    prows = pl.ds(pstart, blk, stride=dil)
                    k = jnp.concatenate([k_ref[0, prows, :], k], axis=0)
                    v = jnp.concatenate([v_ref[0, prows, :], v], axis=0)
                    valid = valid & ((kj >= blk) | (n > 0))
                qq = jnp.concatenate([jnp.where(lo, q, 0.0), jnp.where(lo, 0.0, q)], axis=0)
                units.append((rows, valid, qq.astype(BF16), k.astype(BF16), v))
            scores = [_dot_nt(qq, k) for _, _, qq, k, _ in units]
            probs = []
            for (rows, valid, _, _, _), s in zip(units, scores):
                s = jnp.where(valid, s, -jnp.inf)
                m = jnp.max(s, axis=-1, keepdims=True)
                probs.append((m, jnp.exp(s - m).astype(BF16)))
            accs = [(_dot(p[:blk], jnp.where(lo, v, 1.0)), _dot(p[blk:], jnp.where(lo, 1.0, v)))
                    for (_, _, _, _, v), (_, p) in zip(units, probs)]
            for (rows, _, _, _, _), (m, _), (a0, a1) in zip(units, probs, accs):
                acc_ref[g, 0, rows, :] = a0
                acc_ref[g, 1, rows, :] = a1
                m_ref[g, 0, rows, :] = jnp.broadcast_to(m[:blk], (blk, LANES))
                m_ref[g, 1, rows, :] = jnp.broadcast_to(m[blk:], (blk, LANES))
            return carry

        lax.fori_loop(0, n_units // ATTN_GROUP, group, 0)

    mb = 2 * blk

    def merge(i, carry):
        rows = pl.ds(pl.multiple_of(i * mb, mb), mb)
        out = None
        for e in range(LANES // HEAD_DIM):
            ms = [m_ref[g, e, rows, :] for g in range(len(DIL_BRANCHES))]
            mx = functools.reduce(jnp.maximum, ms)
            tot = None
            for g in range(len(DIL_BRANCHES)):
                term = jnp.exp(ms[g] - mx) * acc_ref[g, e, rows, :]
                tot = term if tot is None else tot + term
            den = pltpu.roll(tot, HEAD_DIM, axis=1)
            oe = tot / den
            out = oe if out is None else jnp.where((lane // HEAD_DIM) == e, oe, out)
        o_ref[0, rows, :] = out.astype(o_ref.dtype)
        return carry

    lax.fori_loop(0, seq // mb, merge, 0, unroll=2)


def _attn_prompt(q, k, v, b, t):
    n_pairs = D_ATT // LANES
    spec = pl.BlockSpec((1, t, LANES), lambda i, j: (i, 0, j))
    return pl.pallas_call(
        functools.partial(_attn_prompt_kernel, seq=t),
        grid=(b, n_pairs),
        in_specs=[spec, spec, spec],
        out_specs=spec,
        out_shape=jax.ShapeDtypeStruct((b, t, D_ATT), BF16),
        scratch_shapes=[pltpu.VMEM((len(DIL_BRANCHES), 2, t, LANES), F32),
                        pltpu.VMEM((len(DIL_BRANCHES), 2, t, LANES), F32)],
        compiler_params=_cparams("parallel", "parallel"),
        name="attn_prompt",
    )(q, k, v)


def _attn_sample_kernel(q_ref, kn_ref, vn_ref, ck_ref, cv_ref, o_ref, *, n_ctx, t_new):
    nh = N_HEADS_ATT
    nq = nh * t_new
    q = q_ref[0]
    qt = jnp.concatenate([q] * nh, axis=0)
    rowh = lax.broadcasted_iota(jnp.int32, (nq, D_ATT), 0) // t_new
    laneh = lax.broadcasted_iota(jnp.int32, (nq, D_ATT), 1) // HEAD_DIM
    own = rowh == laneh
    qbd = jnp.where(own, qt, 0.0)

    pad = LANES - t_new
    kn = jnp.concatenate([kn_ref[0], jnp.zeros((pad, D_ATT), F32)], axis=0)
    vn = jnp.concatenate([vn_ref[0], jnp.zeros((pad, D_ATT), F32)], axis=0)

    s_c = _dot_nt(qbd, ck_ref[0])
    s_n = _dot_nt(qbd, kn)

    def multiplicity(d):
        mult = jnp.zeros(d.shape, F32)
        for window, dil in DIL_BRANCHES:
            mult = mult + jnp.where((d >= 0) & (d <= window) & (d % dil == 0), 1.0, 0.0)
        return mult

    qi_c = lax.broadcasted_iota(jnp.int32, (nq, n_ctx), 0) % t_new
    d_c = n_ctx + qi_c - lax.broadcasted_iota(jnp.int32, (nq, n_ctx), 1)
    qi_n = lax.broadcasted_iota(jnp.int32, (nq, LANES), 0) % t_new
    col_n = lax.broadcasted_iota(jnp.int32, (nq, LANES), 1)
    d_n = jnp.where(col_n < t_new, qi_n - col_n, -1)
    mult_c = multiplicity(d_c)
    mult_n = multiplicity(d_n)

    s_c = jnp.where(mult_c > 0, s_c, -jnp.inf)
    s_n = jnp.where(mult_n > 0, s_n, -jnp.inf)
    mx = jnp.maximum(jnp.max(s_c, axis=-1, keepdims=True), jnp.max(s_n, axis=-1, keepdims=True))
    p_c = mult_c * jnp.exp(s_c - mx)
    p_n = mult_n * jnp.exp(s_n - mx)
    den = jnp.sum(p_c, axis=-1, keepdims=True) + jnp.sum(p_n, axis=-1, keepdims=True)
    o_full = (_dot(p_c, cv_ref[0]) + _dot(p_n, vn)) / den
    o_full = jnp.where(own, o_full, 0.0)
    o = o_full[0:t_new]
    for h in range(1, nh):
        o = o + o_full[h * t_new:(h + 1) * t_new]
    o_ref[0] = o.astype(o_ref.dtype)


def _attn_sample(q, k_new, v_new, cache_k, cache_v):
    b, t_new, _ = q.shape
    n_ctx = cache_k.shape[1]
    new = pl.BlockSpec((1, t_new, D_ATT), lambda i: (i, 0, 0))
    ctx = pl.BlockSpec((1, n_ctx, D_ATT), lambda i: (i, 0, 0))
    return pl.pallas_call(
        functools.partial(_attn_sample_kernel, n_ctx=n_ctx, t_new=t_new),
        grid=(b,),
        in_specs=[new, new, new, ctx, ctx],
        out_specs=new,
        out_shape=jax.ShapeDtypeStruct((b, t_new, D_ATT), BF16),
        compiler_params=_cparams("parallel"),
        name="attn_sample",
    )(q, k_new, v_new, cache_k, cache_v)


def _rwkv_prep_kernel(p_ref, sh_ref, mu_ref, w0_ref, dup_ref, a0_ref, iup_ref, gup_ref, kk_ref, ka_ref,
                      r_o, k_o, v_o, kkn_o, beta_o, logd_o, g_o, carry_ref):
    c1, c2, c3 = D_RWKV, 2 * D_RWKV, 3 * D_RWKV
    c4, c5 = c3 + LORA_PAD, c3 + 2 * LORA_PAD

    @pl.when(pl.program_id(1) == 0)
    def _():
        carry_ref[...] = sh_ref[0]

    cur = p_ref[0]
    bt = cur.shape[0]
    prev = pltpu.roll(cur, 1, axis=0)
    row = lax.broadcasted_iota(jnp.int32, (bt, 1), 0)
    prev = jnp.where(row == 0, carry_ref[...], prev)
    carry_ref[...] = cur[bt - 1:bt]
    xm = cur + (prev - cur) * mu_ref[...]

    r, k, v = xm[:, :c1], xm[:, c1:c2], xm[:, c2:c3]
    wd, ad, gd = xm[:, c3:c4], xm[:, c4:c5], xm[:, c5:]

    z = -(w0_ref[...] + _dot(jnp.tanh(wd), dup_ref[...]))
    softplus = jnp.maximum(z, 0.0) + jnp.log(1.0 + jnp.exp(-jnp.abs(z)))
    w = -softplus - 0.5
    logd_o[0] = -jnp.exp(w)
    a = jax.nn.sigmoid(a0_ref[...] + _dot(ad, iup_ref[...]))
    g_o[0] = _dot(jax.nn.sigmoid(gd), gup_ref[...])

    kk = k * kk_ref[...]
    e = _head_sum_matrix()
    for c in range(N_PAIRS):
        sl = slice(c * LANES, (c + 1) * LANES)
        kc = kk[:, sl]
        ss = _dot_exact_rhs(kc * kc, e)
        kn = kc * lax.rsqrt(jnp.maximum(ss, 1e-24))
        kkn_o[0, :, sl] = kn
        beta_o[0, :, sl] = kn * a[:, sl]
    r_o[0] = r
    k_o[0] = k * (1.0 + (a - 1.0) * ka_ref[...])
    v_o[0] = v


def _rwkv_prep(proj_rw, shift_proj, prm, b, t, bt):
    blk = pl.BlockSpec((1, bt, D_RW_PROJ), lambda i, j: (i, j, 0))
    full = lambda a: pl.BlockSpec(a.shape, lambda i, j: (0,) * a.ndim)
    out = pl.BlockSpec((1, bt, D_RWKV), lambda i, j: (i, j, 0))
    args = (prm["mu"], prm["w0"], prm["decay_up"], prm["a0"], prm["iclr_up"], prm["gate_up"],
            prm["k_k"], prm["k_a"])
    return pl.pallas_call(
        _rwkv_prep_kernel,
        grid=(b, t // bt),
        in_specs=[blk, pl.BlockSpec((1, 1, D_RW_PROJ), lambda i, j: (i, 0, 0))] + [full(a) for a in args],
        out_specs=[out] * 7,
        out_shape=[jax.ShapeDtypeStruct((b, t, D_RWKV), F32)] * 7,
        scratch_shapes=[pltpu.VMEM((1, D_RW_PROJ), F32)],
        compiler_params=_cparams("parallel", "arbitrary"),
        name="rwkv_prep",
    )(proj_rw, shift_proj, *args)


def _rwkv_chunk_kernel(r_ref, k_ref, v_ref, kkn_ref, beta_ref, logd_ref, g_ref, s0_ref,
                       rk_ref, gnw_ref, gnb_ref, o_ref, sout_ref, s_ref):
    c = CHUNK
    ci = pl.program_id(1)

    @pl.when(ci == 0)
    def _():
        s_ref[...] = s0_ref[0]

    tri = (lax.broadcasted_iota(jnp.int32, (c, c), 0) >= lax.broadcasted_iota(jnp.int32, (c, c), 1)).astype(BF16)
    logd = logd_ref[0]
    l1, l2, l3 = _split3(logd)
    cum = (jnp.dot(tri, l1, preferred_element_type=F32) + jnp.dot(tri, l2, preferred_element_type=F32)
           + jnp.dot(tri, l3, preferred_element_type=F32))
    total = cum[c - 1:c]
    e_in = jnp.exp(cum)
    e_ex = jnp.exp(cum - logd)
    e_neg = jnp.exp(-cum)
    e_end = jnp.exp(total - cum)
    p_end = jnp.exp(total)

    r = r_ref[0]
    k = k_ref[0]
    v = v_ref[0]
    beta = beta_ref[0]
    alpha_t = -kkn_ref[0] * e_ex
    r_t = r * e_in
    beta_t = beta * e_neg
    k_t = k * e_neg
    beta_e = beta * e_end
    k_e = k * e_end

    lane = lax.broadcasted_iota(jnp.int32, (1, LANES), 1)
    lo = lane < HEAD_DIM
    tt = lax.broadcasted_iota(jnp.int32, (c, 2 * c), 0)
    ss = lax.broadcasted_iota(jnp.int32, (c, 2 * c), 1) % c
    strict = ss < tt
    incl = ss <= tt
    same_head = (lax.broadcasted_iota(jnp.int32, (LANES, LANES), 0) // HEAD_DIM
                 == lax.broadcasted_iota(jnp.int32, (LANES, LANES), 1) // HEAD_DIM)

    def expand(x):
        return jnp.concatenate([jnp.where(lo, x, 0.0), jnp.where(lo, 0.0, x)], axis=0)

    pairs = range(N_PAIRS)
    sls = [slice(p * LANES, (p + 1) * LANES) for p in pairs]
    v_c = [v[:, sl].astype(BF16) for sl in sls]
    v_x = [expand(v[:, sl]).astype(BF16) for sl in sls]
    lc = [jnp.concatenate([alpha_t[:, sl], r_t[:, sl]], axis=0).astype(BF16) for sl in sls]
    rx = [jnp.concatenate([expand(beta_t[:, sl]), expand(k_t[:, sl])], axis=0).astype(BF16) for sl in sls]
    gl = [_dot_nt(lc[p], jnp.concatenate([rx[p], s_ref[p].astype(BF16)], axis=0)) for p in pairs]
    npow = [jnp.where(strict, gl[p][:c, :2 * c], 0.0).astype(BF16) for p in pairs]
    a_ak = [jnp.where(strict, gl[p][:c, 2 * c:4 * c], 0.0) for p in pairs]
    a_r = [jnp.concatenate([jnp.where(incl, gl[p][c:, :2 * c], 0.0),
                            jnp.where(incl, gl[p][c:, 2 * c:4 * c], 0.0)], axis=1).astype(BF16) for p in pairs]
    u = [gl[p][:c, 4 * c:] + _dot(a_ak[p], v_x[p]) for p in pairs]
    steps = c.bit_length() - 1
    for it in range(steps):
        if it + 1 < steps:
            rhs = [jnp.concatenate([expand(u[p]).astype(BF16), expand(npow[p])], axis=1) for p in pairs]
            both = [_dot(npow[p], rhs[p]) for p in pairs]
            u = [u[p] + both[p][:, :2 * c] for p in pairs]
            npow = [both[p][:, 2 * c:].astype(BF16) for p in pairs]
        else:
            u = [u[p] + _dot(npow[p], expand(u[p])) for p in pairs]
    u_b = [x.astype(BF16) for x in u]
    outs = [gl[p][c:, 4 * c:] + _dot(a_r[p], jnp.concatenate([expand(u_b[p]), v_x[p]], axis=0)) for p in pairs]
    for p in pairs:
        sl = sls[p]
        upd = _dot_tn(jnp.concatenate([u_b[p], v_c[p]], axis=0),
                      jnp.concatenate([beta_e[:, sl], k_e[:, sl]], axis=0))
        s_ref[p] = s_ref[p] * p_end[:, sl] + jnp.where(same_head, upd, 0.0)

    gw = 2 * LANES
    n_grp = D_RWKV // gw
    e = _head_sum_matrix(gw)
    o_all = jnp.concatenate([jnp.concatenate([outs[2 * q], outs[2 * q + 1]], axis=1) for q in range(n_grp)],
                            axis=0)

    def stack(x):
        return jnp.concatenate([x[:, q * gw:(q + 1) * gw] for q in range(n_grp)], axis=0)

    def stack_param(ref):
        x = ref[...]
        return jnp.concatenate([jnp.broadcast_to(x[:, q * gw:(q + 1) * gw], (c, gw)) for q in range(n_grp)],
                               axis=0)

    mean = _dot_exact_rhs(o_all, e) * (1.0 / HEAD_DIM)
    d = o_all - mean
    var = _dot_exact_rhs(d * d, e) * (1.0 / HEAD_DIM)
    y = d * lax.rsqrt(var + GN_EPS) * stack_param(gnw_ref) + stack_param(gnb_ref)
    bonus = _dot_exact_rhs(stack(r * k) * stack_param(rk_ref), e)
    y = (y + bonus * stack(v)) * stack(g_ref[0])
    for q in range(n_grp):
        o_ref[0, :, q * gw:(q + 1) * gw] = y[q * c:(q + 1) * c].astype(o_ref.dtype)

    @pl.when(ci == pl.num_programs(1) - 1)
    def _():
        sout_ref[0] = s_ref[...]


def _rwkv_chunk(r, k, v, kkn, beta, logd, g, s0, prm, b, t):
    blk = pl.BlockSpec((1, CHUNK, D_RWKV), lambda i, j: (i, j, 0))
    st = pl.BlockSpec((1, N_PAIRS, LANES, LANES), lambda i, j: (i, 0, 0, 0))
    par = pl.BlockSpec((1, D_RWKV), lambda i, j: (0, 0))
    return pl.pallas_call(
        _rwkv_chunk_kernel,
        grid=(b, t // CHUNK),
        in_specs=[blk] * 7 + [st, par, par, par],
        out_specs=[blk, st],
        out_shape=[jax.ShapeDtypeStruct((b, t, D_RWKV), BF16),
                   jax.ShapeDtypeStruct((b, N_PAIRS, LANES, LANES), F32)],
        scratch_shapes=[pltpu.VMEM((N_PAIRS, LANES, LANES), F32)],
        compiler_params=_cparams("parallel", "arbitrary"),
        name="rwkv_chunk",
    )(r, k, v, kkn, beta, logd, g, s0, prm["r_k"], prm["gn_w"], prm["gn_b"])


def _state_to_pairs(wkv):
    b = wkv.shape[0]
    w = wkv.reshape(b, N_PAIRS, 2, HEAD_DIM, HEAD_DIM)
    z = jnp.zeros((b, N_PAIRS, HEAD_DIM, HEAD_DIM), wkv.dtype)
    top = jnp.concatenate([w[:, :, 0], z], axis=-1)
    bot = jnp.concatenate([z, w[:, :, 1]], axis=-1)
    return jnp.concatenate([top, bot], axis=-2)


def _pairs_to_state(s):
    b = s.shape[0]
    h0 = s[:, :, :HEAD_DIM, :HEAD_DIM]
    h1 = s[:, :, HEAD_DIM:, HEAD_DIM:]
    return jnp.stack([h0, h1], axis=2).reshape(b, N_HEADS_RWKV, HEAD_DIM, HEAD_DIM)


def _out_router_kernel(x_ref, a_ref, r_ref, wa_ref, wr_ref, g2_ref, rt1_ref, rt2_ref,
                       x2_ref, h2_ref, lg_ref, *, n_real):
    i = pl.program_id(0)

    @pl.when(i < n_real)
    def _():
        x2 = (x_ref[...] + jnp.dot(a_ref[...], wa_ref[...], preferred_element_type=F32)
              + jnp.dot(r_ref[...], wr_ref[...], preferred_element_type=F32))
        x2_ref[...] = x2
        ms = jnp.mean(x2 * x2, axis=-1, keepdims=True)
        h2 = x2 * lax.rsqrt(ms + RMS_EPS) * g2_ref[...]
        hi, lo = _split2(h2)
        h2_ref[...] = h2
        t12 = jnp.dot(hi, rt1_ref[...], preferred_element_type=F32)
        t3 = jnp.dot(lo, rt2_ref[...], preferred_element_type=F32)
        lg_ref[...] = t12[:, :ROUTER_PAD] + t12[:, ROUTER_PAD:] + t3

    @pl.when(i >= n_real)
    def _():
        h2_ref[...] = jnp.zeros_like(h2_ref)
        lg_ref[...] = jnp.zeros_like(lg_ref)


def _out_router(x, o_att, o_rwkv, wa, wr, g2, rt1, rt2, bm, spare_blocks=0):
    m, d = x.shape
    n_real = m // bm
    last = n_real - 1
    row = lambda w: pl.BlockSpec((bm, w), lambda i: (jnp.minimum(i, last), 0))
    grow = lambda w: pl.BlockSpec((bm, w), lambda i: (i, 0))
    full = lambda a: pl.BlockSpec(a.shape, lambda i: (0,) * a.ndim)
    m_out = m + spare_blocks * bm
    return pl.pallas_call(
        functools.partial(_out_router_kernel, n_real=n_real),
        grid=(n_real + spare_blocks,),
        in_specs=[row(d), row(D_ATT), row(D_RWKV), full(wa), full(wr), full(g2), full(rt1), full(rt2)],
        out_specs=[row(d), grow(d), grow(ROUTER_PAD)],
        out_shape=[jax.ShapeDtypeStruct((m, d), F32), jax.ShapeDtypeStruct((m_out, d), F32),
                   jax.ShapeDtypeStruct((m_out, ROUTER_PAD), F32)],
        compiler_params=_cparams("arbitrary"),
        name="out_router",
    )(x, o_att, o_rwkv, wa, wr, g2, rt1, rt2)


def _moe_kernel(te_ref, first_ref, nxt_ref, nv_ref, tok_ref, tok_next_ref, h_hbm, wg_hbm, wu_hbm, wd_hbm,
                y_ref, x_ref, sg_ref, su_ref, sd_ref, wg_ref, wu_ref, wd_ref, sem, xsem):
    i = pl.program_id(0)
    n_valid = nv_ref[0]

    def start_rows(idx_ref, slot, unrolled=False):
        def body(r, carry):
            pltpu.make_async_copy(h_hbm.at[pl.ds(idx_ref[0, 0, r], 1)], x_ref.at[slot, pl.ds(r, 1)],
                                  xsem.at[slot]).start()
            return carry
        if unrolled:
            for r in range(MOE_TILE):
                body(r, 0)
        else:
            lax.fori_loop(0, MOE_TILE, body, 0, unroll=8)

    def wait_rows(slot):
        pltpu.make_async_copy(h_hbm.at[pl.ds(0, MOE_TILE)], x_ref.at[slot], xsem.at[slot]).wait()

    @pl.when(i == 0)
    def _():
        start_rows(tok_ref, 0)

    def weight_copies(e):
        return (pltpu.make_async_copy(wg_hbm.at[e], sg_ref, sem.at[0]),
                pltpu.make_async_copy(wu_hbm.at[e], su_ref, sem.at[1]),
                pltpu.make_async_copy(wd_hbm.at[e], sd_ref, sem.at[2]))

    @pl.when(i == 0)
    def _():
        for cp in weight_copies(te_ref[0]):
            cp.start()

    valid = i < n_valid

    @pl.when(valid & (first_ref[i] == 1))
    def _():
        cps = weight_copies(te_ref[i])
        for cp, stage, dst in zip(cps, (sg_ref, su_ref, sd_ref), (wg_ref, wu_ref, wd_ref)):
            cp.wait()
            dst[...] = stage[...].astype(BF16)

        @pl.when(nxt_ref[i] >= 0)
        def _():
            for cp in weight_copies(nxt_ref[i]):
                cp.start()

    @pl.when(valid)
    def _():
        slot = i % 2
        wait_rows(slot)
        start_rows(tok_next_ref, 1 - slot, unrolled=True)
        x = x_ref[slot].astype(BF16)
        gt = jnp.dot(x, wg_ref[...], preferred_element_type=F32)
        up = jnp.dot(x, wu_ref[...], preferred_element_type=F32)
        hmid = (gt * jax.nn.sigmoid(gt) * up).astype(BF16)
        y_ref[...] = jnp.dot(hmid, wd_ref[...], preferred_element_type=F32)

    @pl.when(jnp.logical_not(valid))
    def _():
        y_ref[...] = jnp.zeros_like(y_ref)

        @pl.when(i == n_valid)
        def _():
            wait_rows(i % 2)


def _moe_ffn(h_rows, buf_tok, tile_expert, tile_first, tile_next, n_valid, wg, wu, wd):
    d = h_rows.shape[1]
    n_tiles = buf_tok.shape[0] // MOE_TILE
    tok3 = buf_tok.reshape(n_tiles, 1, MOE_TILE)
    hbm = pl.BlockSpec(memory_space=pl.ANY)
    grid_spec = pltpu.PrefetchScalarGridSpec(
        num_scalar_prefetch=4,
        grid=(n_tiles,),
        in_specs=[pl.BlockSpec((1, 1, MOE_TILE), lambda i, te, fr, nx, nv: (i, 0, 0), memory_space=pltpu.SMEM),
                  pl.BlockSpec((1, 1, MOE_TILE), lambda i, te, fr, nx, nv: (jnp.minimum(i + 1, n_tiles - 1), 0, 0),
                               memory_space=pltpu.SMEM),
                  hbm, hbm, hbm, hbm],
        out_specs=pl.BlockSpec((MOE_TILE, d), lambda i, te, fr, nx, nv: (i, 0)),
        scratch_shapes=[pltpu.VMEM((2, MOE_TILE, d), F32),
                        pltpu.VMEM((d, D_EXPERT), F32), pltpu.VMEM((d, D_EXPERT), F32),
                        pltpu.VMEM((D_EXPERT, d), F32),
                        pltpu.VMEM((d, D_EXPERT), BF16), pltpu.VMEM((d, D_EXPERT), BF16),
                        pltpu.VMEM((D_EXPERT, d), BF16),
                        pltpu.SemaphoreType.DMA((3,)), pltpu.SemaphoreType.DMA((2,))],
    )
    return pl.pallas_call(
        _moe_kernel,
        grid_spec=grid_spec,
        out_shape=jax.ShapeDtypeStruct((n_tiles * MOE_TILE, d), F32),
        compiler_params=_cparams("arbitrary"),
        name="moe_ffn",
    )(tile_expert, tile_first, tile_next, n_valid, tok3, tok3, h_rows, wg, wu, wd)


def _route(logits):
    lg = logits[:, :N_GROUPS]
    pg = jax.nn.softmax(lg, axis=-1)
    grp = jnp.argmax(lg, axis=-1)
    p_grp = jnp.take_along_axis(pg, grp[:, None], axis=-1)
    le = logits[:, N_GROUPS:N_GROUPS + N_EXPERTS].reshape(-1, N_GROUPS, EXPERTS_PER_GROUP)
    le = jnp.take_along_axis(le, grp[:, None, None], axis=1)[:, 0]
    top_p, top_i = lax.top_k(jax.nn.softmax(le, axis=-1), TOP_K_FINE)
    gate = p_grp * top_p / jnp.sum(top_p, axis=-1, keepdims=True)
    eid = grp[:, None] * EXPERTS_PER_GROUP + top_i
    return eid.astype(jnp.int32), gate


RANK_BLOCK = 256


def _moe(h2, logits, wg, wu, wd):
    n = logits.shape[0]
    eid, gate = _route(logits)
    m = n * TOP_K_FINE
    eid_f = eid.reshape(-1)
    tok = jnp.repeat(jnp.arange(n, dtype=jnp.int32), TOP_K_FINE)
    experts = jnp.arange(N_EXPERTS, dtype=jnp.int32)
    onehot = eid_f[:, None] == experts[None, :]
    nb = m // RANK_BLOCK
    tri = (jnp.arange(RANK_BLOCK)[:, None] >= jnp.arange(RANK_BLOCK)[None, :]).astype(BF16)
    within = jnp.einsum('ij,bjk->bik', tri, onehot.astype(BF16).reshape(nb, RANK_BLOCK, N_EXPERTS),
                        preferred_element_type=F32)
    block_tot = within[:, -1, :]
    block_off = jnp.cumsum(block_tot, axis=0) - block_tot
    incl = (within + block_off[:, None, :]).reshape(m, N_EXPERTS)
    counts = (block_off[-1] + block_tot[-1]).astype(jnp.int32)
    padded = (counts + MOE_TILE - 1) // MOE_TILE * MOE_TILE
    pend = jnp.cumsum(padded)
    pstart = pend - padded
    start = jnp.cumsum(counts) - counts
    dest = jnp.sum(jnp.where(onehot, incl - 1.0 + pstart.astype(F32)[None, :], 0.0), axis=-1).astype(jnp.int32)
    pos = dest.reshape(n, TOP_K_FINE)

    n_tiles = -(-(m + N_EXPERTS * (MOE_TILE - 1)) // MOE_TILE) + 1
    cap = n_tiles * MOE_TILE
    buf_tok = jnp.zeros((cap,), jnp.int32).at[dest].set(tok, unique_indices=True)

    n_valid = (pend[-1] // MOE_TILE).astype(jnp.int32)
    tiles = jnp.arange(n_tiles, dtype=jnp.int32)
    tile_index = jnp.minimum(tiles, n_valid - 1)
    tile_expert = jnp.minimum(jnp.sum((tile_index * MOE_TILE)[:, None] >= pend[None, :], axis=-1),
                              N_EXPERTS - 1).astype(jnp.int32)
    prev_expert = jnp.concatenate([jnp.full((1,), -1, jnp.int32), tile_expert[:-1]])
    tile_first = (tile_expert != prev_expert).astype(jnp.int32)
    next_tile = pend[tile_expert] // MOE_TILE
    tile_next = jnp.where(next_tile < n_valid, tile_expert[jnp.minimum(next_tile, n_tiles - 1)], -1)
    ys = _moe_ffn(h2, buf_tok, tile_expert, tile_first, tile_next.astype(jnp.int32), n_valid.reshape(1),
                  wg, wu, wd)
    return ys, pos, gate


def _combine_kernel(pos_ref, pos_next_ref, x2_ref, gate_ref, ys_hbm, y_ref, rows_ref, sem, *, n_tiles, bt):
    i = pl.program_id(0)
    n_rows = TOP_K_FINE * bt

    def start_rows(idx_ref, slot, unrolled):
        def body(r, carry):
            pltpu.make_async_copy(ys_hbm.at[pl.ds(idx_ref[0, 0, r], 1)], rows_ref.at[slot, pl.ds(r, 1)],
                                  sem.at[slot]).start()
            return carry
        if unrolled:
            for r in range(n_rows):
                body(r, 0)
        else:
            lax.fori_loop(0, n_rows, body, 0, unroll=8)

    @pl.when(i == 0)
    def _():
        start_rows(pos_ref, 0, False)

    slot = i % 2

    @pl.when(i + 1 < n_tiles)
    def _():
        start_rows(pos_next_ref, 1 - slot, True)

    pltpu.make_async_copy(ys_hbm.at[pl.ds(0, n_rows)], rows_ref.at[slot], sem.at[slot]).wait()
    y_ref[...] = (x2_ref[...] + rows_ref[slot, 0:bt] * gate_ref[:, 0:1]
                  + rows_ref[slot, bt:n_rows] * gate_ref[:, 1:2])


def _combine(x2, ys, pos, gate, bt):
    n, d = x2.shape
    n_tiles = n // bt
    pos3 = pos.reshape(n_tiles, bt, TOP_K_FINE).transpose(0, 2, 1).reshape(n_tiles, 1, TOP_K_FINE * bt)
    smem = lambda f: pl.BlockSpec((1, 1, TOP_K_FINE * bt), f, memory_space=pltpu.SMEM)
    return pl.pallas_call(
        functools.partial(_combine_kernel, n_tiles=n_tiles, bt=bt),
        grid=(n_tiles,),
        in_specs=[smem(lambda i: (i, 0, 0)), smem(lambda i: (jnp.minimum(i + 1, n_tiles - 1), 0, 0)),
                  pl.BlockSpec((bt, d), lambda i: (i, 0)), pl.BlockSpec((bt, TOP_K_FINE), lambda i: (i, 0)),
                  pl.BlockSpec(memory_space=pl.ANY)],
        out_specs=pl.BlockSpec((bt, d), lambda i: (i, 0)),
        out_shape=jax.ShapeDtypeStruct((n, d), F32),
        scratch_shapes=[pltpu.VMEM((2, TOP_K_FINE * bt, d), F32), pltpu.SemaphoreType.DMA((2,))],
        compiler_params=_cparams("arbitrary"),
        name="moe_combine",
    )(pos3, pos3, x2, gate, ys)


def _pad_cols(w, width):
    return jnp.pad(w, ((0, 0), (0, width - w.shape[1])))


def _pad_rows(w, height):
    return jnp.pad(w, ((0, height - w.shape[0]), (0, 0)))


def _prepare_params(w_in, tmix_mu, decay_w0, decay_up, iclr_a0, iclr_up, gate_up, k_k, k_a, r_k,
                    gn_w, gn_b, w_out, router_group, router_expert):
    c_att = 3 * D_ATT
    c3 = c_att + 3 * D_RWKV
    c4, c5 = c3 + LORA_DECAY, c3 + LORA_DECAY + LORA_ICLR
    w_rw = jnp.concatenate([w_in[:, c_att:c3], _pad_cols(w_in[:, c3:c4], LORA_PAD),
                            _pad_cols(w_in[:, c4:c5], LORA_PAD), w_in[:, c5:]], axis=1)
    mu = tmix_mu.reshape(1, -1)
    o3 = 3 * D_RWKV
    mu = jnp.concatenate([mu[:, :o3], _pad_cols(mu[:, o3:o3 + LORA_DECAY], LORA_PAD),
                          _pad_cols(mu[:, o3 + LORA_DECAY:o3 + LORA_DECAY + LORA_ICLR], LORA_PAD),
                          mu[:, o3 + LORA_DECAY + LORA_ICLR:]], axis=1)
    router = _pad_cols(jnp.concatenate([router_group, router_expert], axis=1), ROUTER_PAD)
    rt_hi = router.astype(BF16)
    rt_lo = (router - rt_hi.astype(F32)).astype(BF16)
    row = lambda z: z.reshape(1, -1)
    return dict(
        w_att=w_in[:, :c_att].astype(BF16), w_rw=w_rw.astype(BF16), mu=mu,
        w0=row(decay_w0), decay_up=_pad_rows(decay_up, LORA_PAD), a0=row(iclr_a0),
        iclr_up=_pad_rows(iclr_up, LORA_PAD), gate_up=gate_up, k_k=row(k_k), k_a=row(k_a),
        r_k=row(r_k), gn_w=row(gn_w), gn_b=row(gn_b),
        wo_att=w_out[:D_ATT].astype(BF16), wo_rw=w_out[D_ATT:].astype(BF16),
        rt1=jnp.concatenate([rt_hi, rt_lo], axis=1), rt2=rt_hi)


def _mixers(x, pos, shift_row, wkv0, k_ctx, v_ctx, prm, norm1_g, q_norm_g, k_norm_g, bm, bt):
    b, t, d = x.shape
    rows = b * t
    x2d = x.reshape(rows, d)
    g1 = norm1_g.reshape(1, d)
    proj_qk, v = _norm_proj_att(x2d, g1, prm["w_att"], bm=bm)
    proj_qk = proj_qk.reshape(b, t, 2 * D_ATT)
    v = v.reshape(b, t, D_ATT)
    proj_rw = _norm_proj(x2d, g1, prm["w_rw"], normalize=True, bm=bm, bn=D_RW_PROJ // 2,
                         name="proj_rw").reshape(b, t, D_RW_PROJ)
    if shift_row is None:
        shift_proj = jnp.zeros((b, 1, D_RW_PROJ), F32)
    else:
        shift_proj = _norm_proj(shift_row, g1, prm["w_rw"], normalize=False, bm=b, bn=D_RW_PROJ // 2,
                                name="proj_shift").reshape(b, 1, D_RW_PROJ)

    q, k = _qk_prep(proj_qk, pos, q_norm_g, k_norm_g, b, t, bt)
    if k_ctx is None:
        o_att = _attn_prompt(q, k, v, b, t)
    else:
        o_att = _attn_sample(q, k, v, k_ctx, v_ctx)

    r, kr, vr, kkn, beta, logd, g = _rwkv_prep(proj_rw, shift_proj, prm, b, t, bt)
    s0 = _state_to_pairs(wkv0)
    tp = -(-t // CHUNK) * CHUNK
    if tp != t:
        padt = lambda z: jnp.pad(z, ((0, 0), (0, tp - t), (0, 0)))
        r, kr, vr, kkn, beta, logd, g = (padt(z) for z in (r, kr, vr, kkn, beta, logd, g))
    o_rwkv, s_new = _rwkv_chunk(r, kr, vr, kkn, beta, logd, g, s0, prm, b, tp)
    o_rwkv = o_rwkv[:, :t]
    return (o_att.reshape(rows, D_ATT), o_rwkv.reshape(rows, D_RWKV), k, v, _pairs_to_state(s_new))


def kernel(x_prompt, x_sample, cache_k_win, cache_v_win, state_wkv, state_shift, norm1_g, w_in, q_norm_g,
           k_norm_g, tmix_mu, decay_w0, decay_up, iclr_a0, iclr_up, gate_up, k_k, k_a, r_k, gn_w, gn_b,
           w_out, norm2_g, router_group, router_expert, moe_w_gate, moe_w_up, moe_w_down):
    depth = w_in.shape[0]
    assert depth == 1
    l = 0
    b_p, s_p, d = x_prompt.shape
    b_s, t_s, _ = x_sample.shape
    n_ctx = cache_k_win.shape[2]
    prm = _prepare_params(w_in[l], tmix_mu[l], decay_w0[l], decay_up[l], iclr_a0[l], iclr_up[l],
                          gate_up[l], k_k[l], k_a[l], r_k[l], gn_w[l], gn_b[l], w_out[l],
                          router_group[l], router_expert[l])
    g2 = norm2_g[l].reshape(1, d)

    pos_p = jnp.arange(s_p, dtype=F32)
    pos_s = PAST_LEN + jnp.arange(t_s, dtype=F32)
    wkv_zero = jnp.zeros((b_p, N_HEADS_RWKV, HEAD_DIM, HEAD_DIM), F32)

    oa_p, or_p, k_p, v_p, wkv_p = _mixers(x_prompt, pos_p, None, wkv_zero, None, None, prm,
                                          norm1_g[l], q_norm_g[l], k_norm_g[l], bm=512, bt=256)
    ck = cache_k_win[l].reshape(b_s, n_ctx, D_ATT)
    cv = cache_v_win[l].reshape(b_s, n_ctx, D_ATT)
    oa_s, or_s, k_s, v_s, wkv_s = _mixers(x_sample, pos_s, state_shift[l], state_wkv[l], ck, cv, prm,
                                          norm1_g[l], q_norm_g[l], k_norm_g[l], bm=b_s * t_s, bt=t_s)

    n_p, n_s = b_p * s_p, b_s * t_s
    x2_p, h2_p, lg_p = _out_router(x_prompt.reshape(n_p, d), oa_p, or_p, prm["wo_att"], prm["wo_rw"], g2,
                                   prm["rt1"], prm["rt2"], bm=512, spare_blocks=1)
    x2_s, h2_s, lg_s = _out_router(x_sample.reshape(n_s, d), oa_s, or_s, prm["wo_att"], prm["wo_rw"], g2,
                                   prm["rt1"], prm["rt2"], bm=n_s)

    h2 = lax.dynamic_update_slice(h2_p, h2_s, (n_p, 0))
    lg = lax.dynamic_update_slice(lg_p, lg_s, (n_p, 0))[:n_p + n_s]
    ys, pos, gate = _moe(h2, lg, moe_w_gate[l], moe_w_up[l], moe_w_down[l])
    y_p = _combine(x2_p, ys, pos[:n_p], gate[:n_p], bt=MOE_TILE).reshape(b_p, s_p, d)
    y_s = _combine(x2_s, ys, pos[n_p:], gate[n_p:], bt=n_s).reshape(b_s, t_s, d)

    last = jnp.concatenate([x_prompt[:, -1], x_sample[:, -1]], axis=0)
    shift = _rms_rows(last, norm1_g[l].reshape(1, d))

    keep = min(MAX_WINDOW, s_p)
    heads = lambda z, b, t: z.reshape(1, b, t, N_HEADS_ATT, HEAD_DIM)
    return (y_p, y_s,
            heads(k_p[:, s_p - keep:], b_p, keep), heads(v_p[:, s_p - keep:], b_p, keep),
            wkv_p[None], shift[:b_p][None],
            heads(k_s, b_s, t_s), heads(v_s, b_s, t_s), wkv_s[None], shift[b_p:][None])
```

```python
import functools

import jax
import jax.numpy as jnp
from jax import lax
from jax.experimental import pallas as pl
from jax.experimental.pallas import tpu as pltpu

D_MODEL = 2048
HEAD_DIM = 64
N_HEADS_ATT = 12
N_HEADS_RWKV = 20
D_ATT = N_HEADS_ATT * HEAD_DIM
D_RWKV = N_HEADS_RWKV * HEAD_DIM
DIL_BRANCHES = ((128, 1), (512, 4), (2048, 16))
MAX_WINDOW = 2048
BAND_BLOCK = 128
ROT_DIM = HEAD_DIM // 4
ROPE_THETA = 500000.0
LORA_DECAY = 96
LORA_ICLR = 96
LORA_GATE = 256
GN_EPS = 64e-5
RMS_EPS = 1e-6
N_GROUPS = 8
EXPERTS_PER_GROUP = 8
N_EXPERTS = N_GROUPS * EXPERTS_PER_GROUP
TOP_K_FINE = 2
D_EXPERT = 1024
PAST_LEN = 16384

LANES = 128
SUBLANES = 8
VMEM_LIMIT_BYTES = 56 * 1024 * 1024

LORA_PAD = LANES
D_RW_PROJ = 3 * D_RWKV + 2 * LORA_PAD + LORA_GATE
N_PAIRS = D_RWKV // LANES
CHUNK = 64
ROUTER_PAD = LANES
MOE_TILE = 256
ATTN_GROUP = 16

F32 = jnp.float32
BF16 = jnp.bfloat16
HIGHEST = lax.Precision.HIGHEST


def _cparams(*sem):
    return pltpu.CompilerParams(dimension_semantics=("arbitrary",) * len(sem),
                                vmem_limit_bytes=VMEM_LIMIT_BYTES)


def _dot(a, b):
    return jnp.dot(a.astype(BF16), b.astype(BF16), preferred_element_type=F32)


def _dot_nt(a, b):
    return lax.dot_general(a.astype(BF16), b.astype(BF16), (((1,), (1,)), ((), ())),
                           preferred_element_type=F32)


def _dot_tn(a, b):
    return lax.dot_general(a.astype(BF16), b.astype(BF16), (((0,), (0,)), ((), ())),
                           preferred_element_type=F32)


def _split2(x):
    hi = x.astype(BF16)
    lo = (x - hi.astype(F32)).astype(BF16)
    return hi, lo


def _split3(x):
    hi = x.astype(BF16)
    r = x - hi.astype(F32)
    mid = r.astype(BF16)
    lo = (r - mid.astype(F32)).astype(BF16)
    return hi, mid, lo


def _dot_exact_rhs(x, e_bf16):
    hi, lo = _split2(x)
    return (jnp.dot(hi, e_bf16, preferred_element_type=F32)
            + jnp.dot(lo, e_bf16, preferred_element_type=F32))


def _head_sum_matrix(width=LANES):
    r = lax.broadcasted_iota(jnp.int32, (width, width), 0) // HEAD_DIM
    c = lax.broadcasted_iota(jnp.int32, (width, width), 1) // HEAD_DIM
    return (r == c).astype(BF16)


def _proj_kernel(x_ref, g_ref, w_ref, o_ref, h_ref, *, normalize):
    @pl.when(pl.program_id(1) == 0)
    def _():
        x = x_ref[...]
        if normalize:
            ms = jnp.mean(x * x, axis=-1, keepdims=True)
            x = x * lax.rsqrt(ms + RMS_EPS) * g_ref[...]
        h_ref[...] = x.astype(BF16)

    o_ref[...] = jnp.dot(h_ref[...], w_ref[...], preferred_element_type=F32)


def _norm_proj(x, g, w_bf16, *, normalize, bm, bn, name):
    m, d = x.shape
    n = w_bf16.shape[1]
    return pl.pallas_call(
        functools.partial(_proj_kernel, normalize=normalize),
        grid=(m // bm, n // bn),
        in_specs=[pl.BlockSpec((bm, d), lambda i, j: (i, 0)),
                  pl.BlockSpec((1, d), lambda i, j: (0, 0)),
                  pl.BlockSpec((d, bn), lambda i, j: (0, j))],
        out_specs=pl.BlockSpec((bm, bn), lambda i, j: (i, j)),
        out_shape=jax.ShapeDtypeStruct((m, n), F32),
        scratch_shapes=[pltpu.VMEM((bm, d), BF16)],
        compiler_params=_cparams("parallel", "arbitrary"),
        name=name,
    )(x, g, w_bf16)


def _proj_att_kernel(x_ref, g_ref, w_ref, qk_ref, v_ref, h_ref):
    j = pl.program_id(1)

    @pl.when(j == 0)
    def _():
        x = x_ref[...]
        ms = jnp.mean(x * x, axis=-1, keepdims=True)
        h_ref[...] = (x * lax.rsqrt(ms + RMS_EPS) * g_ref[...]).astype(BF16)

    y = jnp.dot(h_ref[...], w_ref[...], preferred_element_type=F32)

    @pl.when(j < 2)
    def _():
        qk_ref[...] = y

    @pl.when(j == 2)
    def _():
        v_ref[...] = y


def _norm_proj_att(x, g, w_bf16, *, bm):
    m, d = x.shape
    return pl.pallas_call(
        _proj_att_kernel,
        grid=(m // bm, 3),
        in_specs=[pl.BlockSpec((bm, d), lambda i, j: (i, 0)),
                  pl.BlockSpec((1, d), lambda i, j: (0, 0)),
                  pl.BlockSpec((d, D_ATT), lambda i, j: (0, j))],
        out_specs=[pl.BlockSpec((bm, D_ATT), lambda i, j: (i, jnp.minimum(j, 1))),
                   pl.BlockSpec((bm, D_ATT), lambda i, j: (i, 0))],
        out_shape=[jax.ShapeDtypeStruct((m, 2 * D_ATT), F32), jax.ShapeDtypeStruct((m, D_ATT), F32)],
        scratch_shapes=[pltpu.VMEM((bm, d), BF16)],
        compiler_params=_cparams("arbitrary", "arbitrary"),
        name="proj_att",
    )(x, g, w_bf16)


def _rms_rows_kernel(x_ref, g_ref, o_ref):
    x = x_ref[...]
    ms = jnp.mean(x * x, axis=-1, keepdims=True)
    o_ref[...] = x * lax.rsqrt(ms + RMS_EPS) * g_ref[...]


def _rms_rows(x, g):
    return pl.pallas_call(
        _rms_rows_kernel,
        out_shape=jax.ShapeDtypeStruct(x.shape, F32),
        name="rms_rows",
    )(x, g)


def _qk_prep_kernel(q_ref, k_ref, cos_ref, sa_ref, sb_ref, qg_ref, kg_ref, qo_ref, ko_ref):
    e = _head_sum_matrix()
    cos = cos_ref[...]
    sa = sa_ref[...]
    sb = sb_ref[...]

    def norm_rope(x, g):
        ss = _dot_exact_rhs(x * x, e)
        y = x * lax.rsqrt(ss * (1.0 / HEAD_DIM) + RMS_EPS) * g
        nxt = pltpu.roll(y, LANES - ROT_DIM // 2, axis=1)
        prv = pltpu.roll(y, ROT_DIM // 2, axis=1)
        return y * cos + nxt * sa + prv * sb

    for c in range(D_ATT // LANES):
        sl = slice(c * LANES, (c + 1) * LANES)
        qo_ref[0, :, sl] = norm_rope(q_ref[0, :, sl], qg_ref[...]) * (HEAD_DIM ** -0.5)
        ko_ref[0, :, sl] = norm_rope(k_ref[0, :, sl], kg_ref[...])


def _rope_tables(pos):
    half = ROT_DIM // 2
    inv_freq = ROPE_THETA ** (-2.0 * jnp.arange(half, dtype=F32) / ROT_DIM)
    ang = pos[:, None] * inv_freq[None, :]
    cos, sin = jnp.cos(ang), jnp.sin(ang)
    t = pos.shape[0]
    one = jnp.ones((t, HEAD_DIM - ROT_DIM), F32)
    zero = jnp.zeros((t, HEAD_DIM - ROT_DIM), F32)
    zh = jnp.zeros((t, half), F32)
    cos_h = jnp.concatenate([cos, cos, one], axis=1)
    sa_h = jnp.concatenate([-sin, zh, zero], axis=1)
    sb_h = jnp.concatenate([zh, sin, zero], axis=1)
    rep = LANES // HEAD_DIM
    return tuple(jnp.tile(z, (1, rep)) for z in (cos_h, sa_h, sb_h))


def _qk_prep(proj_att, pos, q_norm_g, k_norm_g, b, t, bt):
    cos, sa, sb = _rope_tables(pos)
    rep = LANES // HEAD_DIM
    qg = jnp.tile(q_norm_g.reshape(1, HEAD_DIM), (1, rep))
    kg = jnp.tile(k_norm_g.reshape(1, HEAD_DIM), (1, rep))
    tab = pl.BlockSpec((bt, LANES), lambda i, j: (j, 0))
    gsp = pl.BlockSpec((1, LANES), lambda i, j: (0, 0))
    return pl.pallas_call(
        _qk_prep_kernel,
        grid=(b, t // bt),
        in_specs=[pl.BlockSpec((1, bt, D_ATT), lambda i, j: (i, j, 0)),
                  pl.BlockSpec((1, bt, D_ATT), lambda i, j: (i, j, 1)),
                  tab, tab, tab, gsp, gsp],
        out_specs=[pl.BlockSpec((1, bt, D_ATT), lambda i, j: (i, j, 0)),
                   pl.BlockSpec((1, bt, D_ATT), lambda i, j: (i, j, 0))],
        out_shape=[jax.ShapeDtypeStruct((b, t, D_ATT), F32)] * 2,
        compiler_params=_cparams("parallel", "arbitrary"),
        name="qk_prep",
    )(proj_att, proj_att, cos, sa, sb, qg, kg)


def _attn_prompt_kernel(q_ref, k_ref, v_ref, o_ref, acc_ref, m_ref, *, seq):
    blk = BAND_BLOCK
    lane = lax.broadcasted_iota(jnp.int32, (1, LANES), 1)
    lo = lane < HEAD_DIM
    n_units = seq // blk
    qi = lax.broadcasted_iota(jnp.int32, (2 * blk, 1), 0) % blk

    for g, (window, dil) in enumerate(DIL_BRANCHES):
        n_back = window // dil
        n_blk = seq // dil // blk
        with_prev = n_blk > 1
        nk = 2 * blk if with_prev else blk
        kj = lax.broadcasted_iota(jnp.int32, (1, nk), 1)
        dist = qi + (blk if with_prev else 0) - kj
        in_band = (dist >= 0) & (dist <= n_back)

        def group(gi, carry, g=g, dil=dil, with_prev=with_prev, in_band=in_band, kj=kj):
            units = []
            for j in range(ATTN_GROUP):
                u = gi * ATTN_GROUP + j
                r = u % dil
                n = u // dil
                start = r + n * (blk * dil)
                rows = pl.ds(start, blk, stride=dil)
                q = q_ref[0, rows, :]
                k = k_ref[0, rows, :]
                v = v_ref[0, rows, :]
                valid = in_band
                if with_prev:
                    pstart = jnp.maximum(start - blk * dil, r)
                    prows = pl.ds(pstart, blk, stride=dil)
                    k = jnp.concatenate([k_ref[0, prows, :], k], axis=0)
                    v = jnp.concatenate([v_ref[0, prows, :], v], axis=0)
                    valid = valid & ((kj >= blk) | (n > 0))
                qq = jnp.concatenate([jnp.where(lo, q, 0.0), jnp.where(lo, 0.0, q)], axis=0)
                units.append((rows, valid, qq.astype(BF16), k.astype(BF16), v))
            scores = [_dot_nt(qq, k) for _, _, qq, k, _ in units]
            probs = []
            for (rows, valid, _, _, _), s in zip(units, scores):
                s = jnp.where(valid, s, -jnp.inf)
                m = jnp.max(s, axis=-1, keepdims=True)
                probs.append((m, jnp.exp(s - m).astype(BF16)))
            accs = [(_dot(p[:blk], jnp.where(lo, v, 1.0)), _dot(p[blk:], jnp.where(lo, 1.0, v)))
                    for (_, _, _, _, v), (_, p) in zip(units, probs)]
            for (rows, _, _, _, _), (m, _), (a0, a1) in zip(units, probs, accs):
                acc_ref[g, 0, rows, :] = a0
                acc_ref[g, 1, rows, :] = a1
                m_ref[g, 0, rows, :] = jnp.broadcast_to(m[:blk], (blk, LANES))
                m_ref[g, 1, rows, :] = jnp.broadcast_to(m[blk:], (blk, LANES))
            return carry

        lax.fori_loop(0, n_units // ATTN_GROUP, group, 0)

    mb = 2 * blk

    def merge(i, carry):
        rows = pl.ds(pl.multiple_of(i * mb, mb), mb)
        out = None
        for e in range(LANES // HEAD_DIM):
            ms = [m_ref[g, e, rows, :] for g in range(len(DIL_BRANCHES))]
            mx = functools.reduce(jnp.maximum, ms)
            tot = None
            for g in range(len(DIL_BRANCHES)):
                term = jnp.exp(ms[g] - mx) * acc_ref[g, e, rows, :]
                tot = term if tot is None else tot + term
            den = pltpu.roll(tot, HEAD_DIM, axis=1)
            oe = tot / den
            out = oe if out is None else jnp.where((lane // HEAD_DIM) == e, oe, out)
        o_ref[0, rows, :] = out.astype(o_ref.dtype)
        return carry

    lax.fori_loop(0, seq // mb, merge, 0, unroll=2)


def _attn_prompt(q, k, v, b, t):
    n_pairs = D_ATT // LANES
    spec = pl.BlockSpec((1, t, LANES), lambda i, j: (i, 0, j))
    return pl.pallas_call(
        functools.partial(_attn_prompt_kernel, seq=t),
        grid=(b, n_pairs),
        in_specs=[spec, spec, spec],
        out_specs=spec,
        out_shape=jax.ShapeDtypeStruct((b, t, D_ATT), BF16),
        scratch_shapes=[pltpu.VMEM((len(DIL_BRANCHES), 2, t, LANES), F32),
                        pltpu.VMEM((len(DIL_BRANCHES), 2, t, LANES), F32)],
        compiler_params=_cparams("parallel", "parallel"),
        name="attn_prompt",
    )(q, k, v)


def _attn_sample_kernel(q_ref, kn_ref, vn_ref, ck_ref, cv_ref, o_ref, *, n_ctx, t_new):
    nh = N_HEADS_ATT
    nq = nh * t_new
    q = q_ref[0]
    qt = jnp.concatenate([q] * nh, axis=0)
    rowh = lax.broadcasted_iota(jnp.int32, (nq, D_ATT), 0) // t_new
    laneh = lax.broadcasted_iota(jnp.int32, (nq, D_ATT), 1) // HEAD_DIM
    own = rowh == laneh
    qbd = jnp.where(own, qt, 0.0)

    pad = LANES - t_new
    kn = jnp.concatenate([kn_ref[0], jnp.zeros((pad, D_ATT), F32)], axis=0)
    vn = jnp.concatenate([vn_ref[0], jnp.zeros((pad, D_ATT), F32)], axis=0)

    s_c = _dot_nt(qbd, ck_ref[0])
    s_n = _dot_nt(qbd, kn)

    def multiplicity(d):
        mult = jnp.zeros(d.shape, F32)
        for window, dil in DIL_BRANCHES:
            mult = mult + jnp.where((d >= 0) & (d <= window) & (d % dil == 0), 1.0, 0.0)
        return mult

    qi_c = lax.broadcasted_iota(jnp.int32, (nq, n_ctx), 0) % t_new
    d_c = n_ctx + qi_c - lax.broadcasted_iota(jnp.int32, (nq, n_ctx), 1)
    qi_n = lax.broadcasted_iota(jnp.int32, (nq, LANES), 0) % t_new
    col_n = lax.broadcasted_iota(jnp.int32, (nq, LANES), 1)
    d_n = jnp.where(col_n < t_new, qi_n - col_n, -1)
    mult_c = multiplicity(d_c)
    mult_n = multiplicity(d_n)

    s_c = jnp.where(mult_c > 0, s_c, -jnp.inf)
    s_n = jnp.where(mult_n > 0, s_n, -jnp.inf)
    mx = jnp.maximum(jnp.max(s_c, axis=-1, keepdims=True), jnp.max(s_n, axis=-1, keepdims=True))
    p_c = mult_c * jnp.exp(s_c - mx)
    p_n = mult_n * jnp.exp(s_n - mx)
    den = jnp.sum(p_c, axis=-1, keepdims=True) + jnp.sum(p_n, axis=-1, keepdims=True)
    o_full = (_dot(p_c, cv_ref[0]) + _dot(p_n, vn)) / den
    o_full = jnp.where(own, o_full, 0.0)
    o = o_full[0:t_new]
    for h in range(1, nh):
        o = o + o_full[h * t_new:(h + 1) * t_new]
    o_ref[0] = o.astype(o_ref.dtype)


def _attn_sample(q, k_new, v_new, cache_k, cache_v):
    b, t_new, _ = q.shape
    n_ctx = cache_k.shape[1]
    new = pl.BlockSpec((1, t_new, D_ATT), lambda i: (i, 0, 0))
    ctx = pl.BlockSpec((1, n_ctx, D_ATT), lambda i: (i, 0, 0))
    return pl.pallas_call(
        functools.partial(_attn_sample_kernel, n_ctx=n_ctx, t_new=t_new),
        grid=(b,),
        in_specs=[new, new, new, ctx, ctx],
        out_specs=new,
        out_shape=jax.ShapeDtypeStruct((b, t_new, D_ATT), BF16),
        compiler_params=_cparams("parallel"),
        name="attn_sample",
    )(q, k_new, v_new, cache_k, cache_v)


def _rwkv_prep_kernel(p_ref, sh_ref, mu_ref, w0_ref, dup_ref, a0_ref, iup_ref, gup_ref, kk_ref, ka_ref,
                      r_o, k_o, v_o, kkn_o, beta_o, logd_o, g_o, carry_ref):
    c1, c2, c3 = D_RWKV, 2 * D_RWKV, 3 * D_RWKV
    c4, c5 = c3 + LORA_PAD, c3 + 2 * LORA_PAD

    @pl.when(pl.program_id(1) == 0)
    def _():
        carry_ref[...] = sh_ref[0]

    cur = p_ref[0]
    bt = cur.shape[0]
    prev = pltpu.roll(cur, 1, axis=0)
    row = lax.broadcasted_iota(jnp.int32, (bt, 1), 0)
    prev = jnp.where(row == 0, carry_ref[...], prev)
    carry_ref[...] = cur[bt - 1:bt]
    xm = cur + (prev - cur) * mu_ref[...]

    r, k, v = xm[:, :c1], xm[:, c1:c2], xm[:, c2:c3]
    wd, ad, gd = xm[:, c3:c4], xm[:, c4:c5], xm[:, c5:]

    z = -(w0_ref[...] + _dot(jnp.tanh(wd), dup_ref[...]))
    softplus = jnp.maximum(z, 0.0) + jnp.log(1.0 + jnp.exp(-jnp.abs(z)))
    w = -softplus - 0.5
    logd_o[0] = -jnp.exp(w)
    a = jax.nn.sigmoid(a0_ref[...] + _dot(ad, iup_ref[...]))
    g_o[0] = _dot(jax.nn.sigmoid(gd), gup_ref[...])

    kk = k * kk_ref[...]
    e = _head_sum_matrix()
    for c in range(N_PAIRS):
        sl = slice(c * LANES, (c + 1) * LANES)
        kc = kk[:, sl]
        ss = _dot_exact_rhs(kc * kc, e)
        kn = kc * lax.rsqrt(jnp.maximum(ss, 1e-24))
        kkn_o[0, :, sl] = kn
        beta_o[0, :, sl] = kn * a[:, sl]
    r_o[0] = r
    k_o[0] = k * (1.0 + (a - 1.0) * ka_ref[...])
    v_o[0] = v


def _rwkv_prep(proj_rw, shift_proj, prm, b, t, bt):
    blk = pl.BlockSpec((1, bt, D_RW_PROJ), lambda i, j: (i, j, 0))
    full = lambda a: pl.BlockSpec(a.shape, lambda i, j: (0,) * a.ndim)
    out = pl.BlockSpec((1, bt, D_RWKV), lambda i, j: (i, j, 0))
    args = (prm["mu"], prm["w0"], prm["decay_up"], prm["a0"], prm["iclr_up"], prm["gate_up"],
            prm["k_k"], prm["k_a"])
    return pl.pallas_call(
        _rwkv_prep_kernel,
        grid=(b, t // bt),
        in_specs=[blk, pl.BlockSpec((1, 1, D_RW_PROJ), lambda i, j: (i, 0, 0))] + [full(a) for a in args],
        out_specs=[out] * 7,
        out_shape=[jax.ShapeDtypeStruct((b, t, D_RWKV), F32)] * 7,
        scratch_shapes=[pltpu.VMEM((1, D_RW_PROJ), F32)],
        compiler_params=_cparams("parallel", "arbitrary"),
        name="rwkv_prep",
    )(proj_rw, shift_proj, *args)


def _rwkv_chunk_kernel(r_ref, k_ref, v_ref, kkn_ref, beta_ref, logd_ref, g_ref, s0_ref,
                       rk_ref, gnw_ref, gnb_ref, o_ref, sout_ref, s_ref):
    c = CHUNK
    ci = pl.program_id(1)

    @pl.when(ci == 0)
    def _():
        s_ref[...] = s0_ref[0]

    tri = (lax.broadcasted_iota(jnp.int32, (c, c), 0) >= lax.broadcasted_iota(jnp.int32, (c, c), 1)).astype(BF16)
    logd = logd_ref[0]
    l1, l2, l3 = _split3(logd)
    cum = (jnp.dot(tri, l1, preferred_element_type=F32) + jnp.dot(tri, l2, preferred_element_type=F32)
           + jnp.dot(tri, l3, preferred_element_type=F32))
    total = cum[c - 1:c]
    e_in = jnp.exp(cum)
    e_ex = jnp.exp(cum - logd)
    e_neg = jnp.exp(-cum)
    e_end = jnp.exp(total - cum)
    p_end = jnp.exp(total)

    r = r_ref[0]
    k = k_ref[0]
    v = v_ref[0]
    beta = beta_ref[0]
    alpha_t = -kkn_ref[0] * e_ex
    r_t = r * e_in
    beta_t = beta * e_neg
    k_t = k * e_neg
    beta_e = beta * e_end
    k_e = k * e_end

    lane = lax.broadcasted_iota(jnp.int32, (1, LANES), 1)
    lo = lane < HEAD_DIM
    tt = lax.broadcasted_iota(jnp.int32, (c, 2 * c), 0)
    ss = lax.broadcasted_iota(jnp.int32, (c, 2 * c), 1) % c
    strict = ss < tt
    incl = ss <= tt
    same_head = (lax.broadcasted_iota(jnp.int32, (LANES, LANES), 0) // HEAD_DIM
                 == lax.broadcasted_iota(jnp.int32, (LANES, LANES), 1) // HEAD_DIM)

    def expand(x):
        return jnp.concatenate([jnp.where(lo, x, 0.0), jnp.where(lo, 0.0, x)], axis=0)

    pairs = range(N_PAIRS)
    sls = [slice(p * LANES, (p + 1) * LANES) for p in pairs]
    v_c = [v[:, sl].astype(BF16) for sl in sls]
    v_x = [expand(v[:, sl]).astype(BF16) for sl in sls]
    lc = [jnp.concatenate([alpha_t[:, sl], r_t[:, sl]], axis=0).astype(BF16) for sl in sls]
    rx = [jnp.concatenate([expand(beta_t[:, sl]), expand(k_t[:, sl])], axis=0).astype(BF16) for sl in sls]
    gl = [_dot_nt(lc[p], jnp.concatenate([rx[p], s_ref[p].astype(BF16)], axis=0)) for p in pairs]
    npow = [jnp.where(strict, gl[p][:c, :2 * c], 0.0).astype(BF16) for p in pairs]
    a_ak = [jnp.where(strict, gl[p][:c, 2 * c:4 * c], 0.0) for p in pairs]
    a_r = [jnp.concatenate([jnp.where(incl, gl[p][c:, :2 * c], 0.0),
                            jnp.where(incl, gl[p][c:, 2 * c:4 * c], 0.0)], axis=1).astype(BF16) for p in pairs]
    u = [gl[p][:c, 4 * c:] + _dot(a_ak[p], v_x[p]) for p in pairs]
    steps = c.bit_length() - 1
    for it in range(steps):
        if it + 1 < steps:
            rhs = [jnp.concatenate([expand(u[p]).astype(BF16), expand(npow[p])], axis=1) for p in pairs]
            both = [_dot(npow[p], rhs[p]) for p in pairs]
            u = [u[p] + both[p][:, :2 * c] for p in pairs]
            npow = [both[p][:, 2 * c:].astype(BF16) for p in pairs]
        else:
            u = [u[p] + _dot(npow[p], expand(u[p])) for p in pairs]
    u_b = [x.astype(BF16) for x in u]
    outs = [gl[p][c:, 4 * c:] + _dot(a_r[p], jnp.concatenate([expand(u_b[p]), v_x[p]], axis=0)) for p in pairs]
    for p in pairs:
        sl = sls[p]
        upd = _dot_tn(jnp.concatenate([u_b[p], v_c[p]], axis=0),
                      jnp.concatenate([beta_e[:, sl], k_e[:, sl]], axis=0))
        s_ref[p] = s_ref[p] * p_end[:, sl] + jnp.where(same_head, upd, 0.0)

    gw = 2 * LANES
    n_grp = D_RWKV // gw
    e = _head_sum_matrix(gw)
    o_all = jnp.concatenate([jnp.concatenate([outs[2 * q], outs[2 * q + 1]], axis=1) for q in range(n_grp)],
                            axis=0)

    def stack(x):
        return jnp.concatenate([x[:, q * gw:(q + 1) * gw] for q in range(n_grp)], axis=0)

    def stack_param(ref):
        x = ref[...]
        return jnp.concatenate([jnp.broadcast_to(x[:, q * gw:(q + 1) * gw], (c, gw)) for q in range(n_grp)],
                               axis=0)

    mean = _dot_exact_rhs(o_all, e) * (1.0 / HEAD_DIM)
    d = o_all - mean
    var = _dot_exact_rhs(d * d, e) * (1.0 / HEAD_DIM)
    y = d * lax.rsqrt(var + GN_EPS) * stack_param(gnw_ref) + stack_param(gnb_ref)
    bonus = _dot_exact_rhs(stack(r * k) * stack_param(rk_ref), e)
    y = (y + bonus * stack(v)) * stack(g_ref[0])
    for q in range(n_grp):
        o_ref[0, :, q * gw:(q + 1) * gw] = y[q * c:(q + 1) * c].astype(o_ref.dtype)

    @pl.when(ci == pl.num_programs(1) - 1)
    def _():
        sout_ref[0] = s_ref[...]


def _rwkv_chunk(r, k, v, kkn, beta, logd, g, s0, prm, b, t):
    blk = pl.BlockSpec((1, CHUNK, D_RWKV), lambda i, j: (i, j, 0))
    st = pl.BlockSpec((1, N_PAIRS, LANES, LANES), lambda i, j: (i, 0, 0, 0))
    par = pl.BlockSpec((1, D_RWKV), lambda i, j: (0, 0))
    return pl.pallas_call(
        _rwkv_chunk_kernel,
        grid=(b, t // CHUNK),
        in_specs=[blk] * 7 + [st, par, par, par],
        out_specs=[blk, st],
        out_shape=[jax.ShapeDtypeStruct((b, t, D_RWKV), BF16),
                   jax.ShapeDtypeStruct((b, N_PAIRS, LANES, LANES), F32)],
        scratch_shapes=[pltpu.VMEM((N_PAIRS, LANES, LANES), F32)],
        compiler_params=_cparams("parallel", "arbitrary"),
        name="rwkv_chunk",
    )(r, k, v, kkn, beta, logd, g, s0, prm["r_k"], prm["gn_w"], prm["gn_b"])


def _state_to_pairs(wkv):
    b = wkv.shape[0]
    w = wkv.reshape(b, N_PAIRS, 2, HEAD_DIM, HEAD_DIM)
    z = jnp.zeros((b, N_PAIRS, HEAD_DIM, HEAD_DIM), wkv.dtype)
    top = jnp.concatenate([w[:, :, 0], z], axis=-1)
    bot = jnp.concatenate([z, w[:, :, 1]], axis=-1)
    return jnp.concatenate([top, bot], axis=-2)


def _pairs_to_state(s):
    b = s.shape[0]
    h0 = s[:, :, :HEAD_DIM, :HEAD_DIM]
    h1 = s[:, :, HEAD_DIM:, HEAD_DIM:]
    return jnp.stack([h0, h1], axis=2).reshape(b, N_HEADS_RWKV, HEAD_DIM, HEAD_DIM)


def _out_router_kernel(x_ref, a_ref, r_ref, wa_ref, wr_ref, g2_ref, rt1_ref, rt2_ref,
                       x2_ref, h2_ref, lg_ref, *, n_real):
    i = pl.program_id(0)

    @pl.when(i < n_real)
    def _():
        x2 = (x_ref[...] + jnp.dot(a_ref[...], wa_ref[...], preferred_element_type=F32)
              + jnp.dot(r_ref[...], wr_ref[...], preferred_element_type=F32))
        x2_ref[...] = x2
        ms = jnp.mean(x2 * x2, axis=-1, keepdims=True)
        h2 = x2 * lax.rsqrt(ms + RMS_EPS) * g2_ref[...]
        hi, lo = _split2(h2)
        h2_ref[...] = h2
        t12 = jnp.dot(hi, rt1_ref[...], preferred_element_type=F32)
        t3 = jnp.dot(lo, rt2_ref[...], preferred_element_type=F32)
        lg_ref[...] = t12[:, :ROUTER_PAD] + t12[:, ROUTER_PAD:] + t3

    @pl.when(i >= n_real)
    def _():
        h2_ref[...] = jnp.zeros_like(h2_ref)
        lg_ref[...] = jnp.zeros_like(lg_ref)


def _out_router(x, o_att, o_rwkv, wa, wr, g2, rt1, rt2, bm, spare_blocks=0):
    m, d = x.shape
    n_real = m // bm
    last = n_real - 1
    row = lambda w: pl.BlockSpec((bm, w), lambda i: (jnp.minimum(i, last), 0))
    grow = lambda w: pl.BlockSpec((bm, w), lambda i: (i, 0))
    full = lambda a: pl.BlockSpec(a.shape, lambda i: (0,) * a.ndim)
    m_out = m + spare_blocks * bm
    return pl.pallas_call(
        functools.partial(_out_router_kernel, n_real=n_real),
        grid=(n_real + spare_blocks,),
        in_specs=[row(d), row(D_ATT), row(D_RWKV), full(wa), full(wr), full(g2), full(rt1), full(rt2)],
        out_specs=[row(d), grow(d), grow(ROUTER_PAD)],
        out_shape=[jax.ShapeDtypeStruct((m, d), F32), jax.ShapeDtypeStruct((m_out, d), F32),
                   jax.ShapeDtypeStruct((m_out, ROUTER_PAD), F32)],
        compiler_params=_cparams("arbitrary"),
        name="out_router",
    )(x, o_att, o_rwkv, wa, wr, g2, rt1, rt2)


def _moe_kernel(te_ref, first_ref, nxt_ref, nv_ref, tok_ref, tok_next_ref, h_hbm, wg_hbm, wu_hbm, wd_hbm,
                y_ref, x_ref, sg_ref, su_ref, sd_ref, wg_ref, wu_ref, wd_ref, sem, xsem):
    i = pl.program_id(0)
    n_valid = nv_ref[0]

    def start_rows(idx_ref, slot, unrolled=False):
        def body(r, carry):
            pltpu.make_async_copy(h_hbm.at[pl.ds(idx_ref[0, 0, r], 1)], x_ref.at[slot, pl.ds(r, 1)],
                                  xsem.at[slot]).start()
            return carry
        if unrolled:
            for r in range(MOE_TILE):
                body(r, 0)
        else:
            lax.fori_loop(0, MOE_TILE, body, 0, unroll=8)

    def wait_rows(slot):
        pltpu.make_async_copy(h_hbm.at[pl.ds(0, MOE_TILE)], x_ref.at[slot], xsem.at[slot]).wait()

    @pl.when(i == 0)
    def _():
        start_rows(tok_ref, 0)

    def weight_copies(e):
        return (pltpu.make_async_copy(wg_hbm.at[e], sg_ref, sem.at[0]),
                pltpu.make_async_copy(wu_hbm.at[e], su_ref, sem.at[1]),
                pltpu.make_async_copy(wd_hbm.at[e], sd_ref, sem.at[2]))

    @pl.when(i == 0)
    def _():
        for cp in weight_copies(te_ref[0]):
            cp.start(priority=1)

    valid = i < n_valid

    @pl.when(valid & (first_ref[i] == 1))
    def _():
        cps = weight_copies(te_ref[i])
        for cp, stage, dst in zip(cps, (sg_ref, su_ref, sd_ref), (wg_ref, wu_ref, wd_ref)):
            cp.wait()
            dst[...] = stage[...].astype(BF16)

        @pl.when(nxt_ref[i] >= 0)
        def _():
            for cp in weight_copies(nxt_ref[i]):
                cp.start(priority=1)

    @pl.when(valid)
    def _():
        slot = i % 2
        wait_rows(slot)
        start_rows(tok_next_ref, 1 - slot, unrolled=True)
        x = x_ref[slot].astype(BF16)
        gt = jnp.dot(x, wg_ref[...], preferred_element_type=F32)
        up = jnp.dot(x, wu_ref[...], preferred_element_type=F32)
        hmid = (gt * jax.nn.sigmoid(gt) * up).astype(BF16)
        y_ref[...] = jnp.dot(hmid, wd_ref[...], preferred_element_type=F32)

    @pl.when(jnp.logical_not(valid))
    def _():
        y_ref[...] = jnp.zeros_like(y_ref)

        @pl.when(i == n_valid)
        def _():
            wait_rows(i % 2)


def _moe_ffn(h_rows, buf_tok, tile_expert, tile_first, tile_next, n_valid, wg, wu, wd):
    d = h_rows.shape[1]
    n_tiles = buf_tok.shape[0] // MOE_TILE
    tok3 = buf_tok.reshape(n_tiles, 1, MOE_TILE)
    hbm = pl.BlockSpec(memory_space=pl.ANY)
    grid_spec = pltpu.PrefetchScalarGridSpec(
        num_scalar_prefetch=4,
        grid=(n_tiles,),
        in_specs=[pl.BlockSpec((1, 1, MOE_TILE), lambda i, te, fr, nx, nv: (i, 0, 0), memory_space=pltpu.SMEM),
                  pl.BlockSpec((1, 1, MOE_TILE), lambda i, te, fr, nx, nv: (jnp.minimum(i + 1, n_tiles - 1), 0, 0),
                               memory_space=pltpu.SMEM),
                  hbm, hbm, hbm, hbm],
        out_specs=pl.BlockSpec((MOE_TILE, d), lambda i, te, fr, nx, nv: (i, 0)),
        scratch_shapes=[pltpu.VMEM((2, MOE_TILE, d), F32),
                        pltpu.VMEM((d, D_EXPERT), F32), pltpu.VMEM((d, D_EXPERT), F32),
                        pltpu.VMEM((D_EXPERT, d), F32),
                        pltpu.VMEM((d, D_EXPERT), BF16), pltpu.VMEM((d, D_EXPERT), BF16),
                        pltpu.VMEM((D_EXPERT, d), BF16),
                        pltpu.SemaphoreType.DMA((3,)), pltpu.SemaphoreType.DMA((2,))],
    )
    return pl.pallas_call(
        _moe_kernel,
        grid_spec=grid_spec,
        out_shape=jax.ShapeDtypeStruct((n_tiles * MOE_TILE, d), F32),
        compiler_params=_cparams("arbitrary"),
        name="moe_ffn",
    )(tile_expert, tile_first, tile_next, n_valid, tok3, tok3, h_rows, wg, wu, wd)


def _route(logits):
    lg = logits[:, :N_GROUPS]
    pg = jax.nn.softmax(lg, axis=-1)
    grp = jnp.argmax(lg, axis=-1)
    p_grp = jnp.take_along_axis(pg, grp[:, None], axis=-1)
    le = logits[:, N_GROUPS:N_GROUPS + N_EXPERTS].reshape(-1, N_GROUPS, EXPERTS_PER_GROUP)
    le = jnp.take_along_axis(le, grp[:, None, None], axis=1)[:, 0]
    top_p, top_i = lax.top_k(jax.nn.softmax(le, axis=-1), TOP_K_FINE)
    gate = p_grp * top_p / jnp.sum(top_p, axis=-1, keepdims=True)
    eid = grp[:, None] * EXPERTS_PER_GROUP + top_i
    return eid.astype(jnp.int32), gate


RANK_BLOCK = 256


def _moe(h2, logits, wg, wu, wd):
    n = logits.shape[0]
    eid, gate = _route(logits)
    m = n * TOP_K_FINE
    eid_f = eid.reshape(-1)
    tok = jnp.repeat(jnp.arange(n, dtype=jnp.int32), TOP_K_FINE)
    experts = jnp.arange(N_EXPERTS, dtype=jnp.int32)
    onehot = eid_f[:, None] == experts[None, :]
    nb = m // RANK_BLOCK
    tri = (jnp.arange(RANK_BLOCK)[:, None] >= jnp.arange(RANK_BLOCK)[None, :]).astype(BF16)
    within = jnp.einsum('ij,bjk->bik', tri, onehot.astype(BF16).reshape(nb, RANK_BLOCK, N_EXPERTS),
                        preferred_element_type=F32)
    block_tot = within[:, -1, :]
    block_off = jnp.cumsum(block_tot, axis=0) - block_tot
    incl = (within + block_off[:, None, :]).reshape(m, N_EXPERTS)
    counts = (block_off[-1] + block_tot[-1]).astype(jnp.int32)
    padded = (counts + MOE_TILE - 1) // MOE_TILE * MOE_TILE
    pend = jnp.cumsum(padded)
    pstart = pend - padded
    start = jnp.cumsum(counts) - counts
    dest = jnp.sum(jnp.where(onehot, incl - 1.0 + pstart.astype(F32)[None, :], 0.0), axis=-1).astype(jnp.int32)
    pos = dest.reshape(n, TOP_K_FINE)

    n_tiles = -(-(m + N_EXPERTS * (MOE_TILE - 1)) // MOE_TILE) + 1
    cap = n_tiles * MOE_TILE
    buf_tok = jnp.zeros((cap,), jnp.int32).at[dest].set(tok, unique_indices=True)

    n_valid = (pend[-1] // MOE_TILE).astype(jnp.int32)
    tiles = jnp.arange(n_tiles, dtype=jnp.int32)
    tile_index = jnp.minimum(tiles, n_valid - 1)
    tile_expert = jnp.minimum(jnp.sum((tile_index * MOE_TILE)[:, None] >= pend[None, :], axis=-1),
                              N_EXPERTS - 1).astype(jnp.int32)
    prev_expert = jnp.concatenate([jnp.full((1,), -1, jnp.int32), tile_expert[:-1]])
    tile_first = (tile_expert != prev_expert).astype(jnp.int32)
    next_tile = pend[tile_expert] // MOE_TILE
    tile_next = jnp.where(next_tile < n_valid, tile_expert[jnp.minimum(next_tile, n_tiles - 1)], -1)
    ys = _moe_ffn(h2, buf_tok, tile_expert, tile_first, tile_next.astype(jnp.int32), n_valid.reshape(1),
                  wg, wu, wd)
    return ys, pos, gate


def _combine_kernel(pos_ref, pos_next_ref, x2_ref, gate_ref, ys_hbm, y_ref, rows_ref, sem, *, n_tiles, bt):
    i = pl.program_id(0)
    n_rows = TOP_K_FINE * bt

    def start_rows(idx_ref, slot, unrolled):
        def body(r, carry, priority=0):
            pltpu.make_async_copy(ys_hbm.at[pl.ds(idx_ref[0, 0, r], 1)], rows_ref.at[slot, pl.ds(r, 1)],
                                  sem.at[slot]).start(priority=priority)
            return carry
        if unrolled:
            for r in range(n_rows):
                body(r, 0, r % 2)
        else:
            lax.fori_loop(0, n_rows, body, 0, unroll=8)

    @pl.when(i == 0)
    def _():
        start_rows(pos_ref, 0, False)

    slot = i % 2

    @pl.when(i + 1 < n_tiles)
    def _():
        start_rows(pos_next_ref, 1 - slot, True)

    pltpu.make_async_copy(ys_hbm.at[pl.ds(0, n_rows)], rows_ref.at[slot], sem.at[slot]).wait()
    y_ref[...] = (x2_ref[...] + rows_ref[slot, 0:bt] * gate_ref[:, 0:1]
                  + rows_ref[slot, bt:n_rows] * gate_ref[:, 1:2])


def _combine(x2, ys, pos, gate, bt):
    n, d = x2.shape
    n_tiles = n // bt
    pos3 = pos.reshape(n_tiles, bt, TOP_K_FINE).transpose(0, 2, 1).reshape(n_tiles, 1, TOP_K_FINE * bt)
    smem = lambda f: pl.BlockSpec((1, 1, TOP_K_FINE * bt), f, memory_space=pltpu.SMEM)
    return pl.pallas_call(
        functools.partial(_combine_kernel, n_tiles=n_tiles, bt=bt),
        grid=(n_tiles,),
        in_specs=[smem(lambda i: (i, 0, 0)), smem(lambda i: (jnp.minimum(i + 1, n_tiles - 1), 0, 0)),
                  pl.BlockSpec((bt, d), lambda i: (i, 0)), pl.BlockSpec((bt, TOP_K_FINE), lambda i: (i, 0)),
                  pl.BlockSpec(memory_space=pl.ANY)],
        out_specs=pl.BlockSpec((bt, d), lambda i: (i, 0)),
        out_shape=jax.ShapeDtypeStruct((n, d), F32),
        scratch_shapes=[pltpu.VMEM((2, TOP_K_FINE * bt, d), F32), pltpu.SemaphoreType.DMA((2,))],
        compiler_params=_cparams("arbitrary"),
        name="moe_combine",
    )(pos3, pos3, x2, gate, ys)


def _pad_cols(w, width):
    return jnp.pad(w, ((0, 0), (0, width - w.shape[1])))


def _pad_rows(w, height):
    return jnp.pad(w, ((0, height - w.shape[0]), (0, 0)))


def _prepare_params(w_in, tmix_mu, decay_w0, decay_up, iclr_a0, iclr_up, gate_up, k_k, k_a, r_k,
                    gn_w, gn_b, w_out, router_group, router_expert):
    c_att = 3 * D_ATT
    c3 = c_att + 3 * D_RWKV
    c4, c5 = c3 + LORA_DECAY, c3 + LORA_DECAY + LORA_ICLR
    w_rw = jnp.concatenate([w_in[:, c_att:c3], _pad_cols(w_in[:, c3:c4], LORA_PAD),
                            _pad_cols(w_in[:, c4:c5], LORA_PAD), w_in[:, c5:]], axis=1)
    mu = tmix_mu.reshape(1, -1)
    o3 = 3 * D_RWKV
    mu = jnp.concatenate([mu[:, :o3], _pad_cols(mu[:, o3:o3 + LORA_DECAY], LORA_PAD),
                          _pad_cols(mu[:, o3 + LORA_DECAY:o3 + LORA_DECAY + LORA_ICLR], LORA_PAD),
                          mu[:, o3 + LORA_DECAY + LORA_ICLR:]], axis=1)
    router = _pad_cols(jnp.concatenate([router_group, router_expert], axis=1), ROUTER_PAD)
    rt_hi = router.astype(BF16)
    rt_lo = (router - rt_hi.astype(F32)).astype(BF16)
    row = lambda z: z.reshape(1, -1)
    return dict(
        w_att=w_in[:, :c_att].astype(BF16), w_rw=w_rw.astype(BF16), mu=mu,
        w0=row(decay_w0), decay_up=_pad_rows(decay_up, LORA_PAD), a0=row(iclr_a0),
        iclr_up=_pad_rows(iclr_up, LORA_PAD), gate_up=gate_up, k_k=row(k_k), k_a=row(k_a),
        r_k=row(r_k), gn_w=row(gn_w), gn_b=row(gn_b),
        wo_att=w_out[:D_ATT].astype(BF16), wo_rw=w_out[D_ATT:].astype(BF16),
        rt1=jnp.concatenate([rt_hi, rt_lo], axis=1), rt2=rt_hi)


def _mixers(x, pos, shift_row, wkv0, k_ctx, v_ctx, prm, norm1_g, q_norm_g, k_norm_g, bm, bt):
    b, t, d = x.shape
    rows = b * t
    x2d = x.reshape(rows, d)
    g1 = norm1_g.reshape(1, d)
    proj_qk, v = _norm_proj_att(x2d, g1, prm["w_att"], bm=bm)
    proj_qk = proj_qk.reshape(b, t, 2 * D_ATT)
    v = v.reshape(b, t, D_ATT)
    proj_rw = _norm_proj(x2d, g1, prm["w_rw"], normalize=True, bm=bm, bn=D_RW_PROJ // 2,
                         name="proj_rw").reshape(b, t, D_RW_PROJ)
    if shift_row is None:
        shift_proj = jnp.zeros((b, 1, D_RW_PROJ), F32)
    else:
        shift_proj = _norm_proj(shift_row, g1, prm["w_rw"], normalize=False, bm=b, bn=D_RW_PROJ // 2,
                                name="proj_shift").reshape(b, 1, D_RW_PROJ)

    q, k = _qk_prep(proj_qk, pos, q_norm_g, k_norm_g, b, t, bt)
    if k_ctx is None:
        o_att = _attn_prompt(q, k, v, b, t)
    else:
        o_att = _attn_sample(q, k, v, k_ctx, v_ctx)

    r, kr, vr, kkn, beta, logd, g = _rwkv_prep(proj_rw, shift_proj, prm, b, t, bt)
    s0 = _state_to_pairs(wkv0)
    tp = -(-t // CHUNK) * CHUNK
    if tp != t:
        padt = lambda z: jnp.pad(z, ((0, 0), (0, tp - t), (0, 0)))
        r, kr, vr, kkn, beta, logd, g = (padt(z) for z in (r, kr, vr, kkn, beta, logd, g))
    o_rwkv, s_new = _rwkv_chunk(r, kr, vr, kkn, beta, logd, g, s0, prm, b, tp)
    o_rwkv = o_rwkv[:, :t]
    return (o_att.reshape(rows, D_ATT), o_rwkv.reshape(rows, D_RWKV), k, v, _pairs_to_state(s_new))


def kernel(x_prompt, x_sample, cache_k_win, cache_v_win, state_wkv, state_shift, norm1_g, w_in, q_norm_g,
           k_norm_g, tmix_mu, decay_w0, decay_up, iclr_a0, iclr_up, gate_up, k_k, k_a, r_k, gn_w, gn_b,
           w_out, norm2_g, router_group, router_expert, moe_w_gate, moe_w_up, moe_w_down):
    depth = w_in.shape[0]
    assert depth == 1
    l = 0
    b_p, s_p, d = x_prompt.shape
    b_s, t_s, _ = x_sample.shape
    n_ctx = cache_k_win.shape[2]
    prm = _prepare_params(w_in[l], tmix_mu[l], decay_w0[l], decay_up[l], iclr_a0[l], iclr_up[l],
                          gate_up[l], k_k[l], k_a[l], r_k[l], gn_w[l], gn_b[l], w_out[l],
                          router_group[l], router_expert[l])
    g2 = norm2_g[l].reshape(1, d)

    pos_p = jnp.arange(s_p, dtype=F32)
    pos_s = PAST_LEN + jnp.arange(t_s, dtype=F32)
    wkv_zero = jnp.zeros((b_p, N_HEADS_RWKV, HEAD_DIM, HEAD_DIM), F32)

    oa_p, or_p, k_p, v_p, wkv_p = _mixers(x_prompt, pos_p, None, wkv_zero, None, None, prm,
                                          norm1_g[l], q_norm_g[l], k_norm_g[l], bm=512, bt=256)
    ck = cache_k_win[l].reshape(b_s, n_ctx, D_ATT)
    cv = cache_v_win[l].reshape(b_s, n_ctx, D_ATT)
    oa_s, or_s, k_s, v_s, wkv_s = _mixers(x_sample, pos_s, state_shift[l], state_wkv[l], ck, cv, prm,
                                          norm1_g[l], q_norm_g[l], k_norm_g[l], bm=b_s * t_s, bt=t_s)

    n_p, n_s = b_p * s_p, b_s * t_s
    x2_p, h2_p, lg_p = _out_router(x_prompt.reshape(n_p, d), oa_p, or_p, prm["wo_att"], prm["wo_rw"], g2,
                                   prm["rt1"], prm["rt2"], bm=512, spare_blocks=1)
    x2_s, h2_s, lg_s = _out_router(x_sample.reshape(n_s, d), oa_s, or_s, prm["wo_att"], prm["wo_rw"], g2,
                                   prm["rt1"], prm["rt2"], bm=n_s)

    h2 = lax.dynamic_update_slice(h2_p, h2_s, (n_p, 0))
    lg = lax.dynamic_update_slice(lg_p, lg_s, (n_p, 0))[:n_p + n_s]
    ys, pos, gate = _moe(h2, lg, moe_w_gate[l], moe_w_up[l], moe_w_down[l])
    y_p = _combine(x2_p, ys, pos[:n_p], gate[:n_p], bt=MOE_TILE).reshape(b_p, s_p, d)
    y_s = _combine(x2_s, ys, pos[n_p:], gate[n_p:], bt=n_s).reshape(b_s, t_s, d)

    last = jnp.concatenate([x_prompt[:, -1], x_sample[:, -1]], axis=0)
    shift = _rms_rows(last, norm1_g[l].reshape(1, d))

    keep = min(MAX_WINDOW, s_p)
    heads = lambda z, b, t: z.reshape(1, b, t, N_HEADS_ATT, HEAD_DIM)
    return (y_p, y_s,
            heads(k_p[:, s_p - keep:], b_p, keep), heads(v_p[:, s_p - keep:], b_p, keep),
            wkv_p[None], shift[:b_p][None],
            heads(k_s, b_s, t_s), heads(v_s, b_s, t_s), wkv_s[None], shift[b_p:][None])
```
